```python
import math
import jax, jax.numpy as jnp
from jax import lax
import numpy as np

D_MODEL = 2048
BATCH = 2
SEQ = 4096
DEPTH = 1
DEC_BATCH = 128
DEC_SEQ = 8
PAST_LEN = 2048
PAGE_SIZE = 128

D_MIX = D_MODEL
FOX_HEAD_DIM = 128
FOX_WIDTH = D_MIX // 2
FOX_HEADS = FOX_WIDTH // FOX_HEAD_DIM
FOX_SCALE = FOX_HEAD_DIM ** -0.5
Q_BLOCK = 128
POOL_WINDOWS = (2, 4, 8, 16)
POOL_GROUPS = len(POOL_WINDOWS)
POOL_WIDTH = D_MIX - FOX_WIDTH
POOL_GROUP_DIM = POOL_WIDTH // POOL_GROUPS
POOL_STATE = max(POOL_WINDOWS) - 1
IN_COLS = 3 * FOX_WIDTH + FOX_HEADS + POOL_WIDTH
N_MEM = 256
XATTN_HEADS = 4
XATTN_HEAD_DIM = 128
XATTN_WIDTH = XATTN_HEADS * XATTN_HEAD_DIM
XATTN_SCALE = XATTN_HEAD_DIM ** -0.5
N_EXPERTS = 32
TOP_K = 4
D_EXPERT = D_MODEL
SWIGLU_ALPHA = 1.702
SWIGLU_LIMIT = 7.0
EXPERT_BLOCK = 128
NORM_EPS = 1e-5

kernel_name = 'fox_pool_hymba_moe_decode_step'


def _rms(x, g):
    xf = x.astype(jnp.float32)
    y = xf * lax.rsqrt(jnp.mean(xf * xf, axis=-1, keepdims=True) + NORM_EPS)
    return (y * g.astype(jnp.float32)).astype(x.dtype)


def _split_mix(h, b_forget):
    bsz, t = h.shape[:2]
    q = h[..., :FOX_WIDTH].reshape(bsz, t, FOX_HEADS, FOX_HEAD_DIM)
    k = h[..., FOX_WIDTH:2 * FOX_WIDTH].reshape(bsz, t, FOX_HEADS, FOX_HEAD_DIM)
    v = h[..., 2 * FOX_WIDTH:3 * FOX_WIDTH].reshape(bsz, t, FOX_HEADS, FOX_HEAD_DIM)
    f = h[..., 3 * FOX_WIDTH:3 * FOX_WIDTH + FOX_HEADS]
    u = h[..., 3 * FOX_WIDTH + FOX_HEADS:]
    logf = jax.nn.log_sigmoid(f.astype(jnp.float32) + b_forget.astype(jnp.float32))
    return q, k, v, logf, u


def _fox_attend(q, k, v, c_q, c_k, q_pos, k_pos):
    s = jnp.einsum('bqhd,bkhd->bhqk', q, k, preferred_element_type=jnp.float32) * FOX_SCALE
    decay = jnp.swapaxes(c_q, 1, 2)[:, :, :, None] - jnp.swapaxes(c_k, 1, 2)[:, :, None, :]
    mask = k_pos[None, :] <= q_pos[:, None]
    p = jax.nn.softmax(jnp.where(mask, s + decay, -jnp.inf), axis=-1)
    return jnp.einsum('bhqk,bkhd->bqhd', p.astype(v.dtype), v)


def _fox_prompt(q, k, v, c):
    bsz, t = q.shape[:2]
    nb = t // Q_BLOCK
    qb = jnp.swapaxes(q.reshape(bsz, nb, Q_BLOCK, FOX_HEADS, FOX_HEAD_DIM), 0, 1)
    cb = jnp.swapaxes(c.reshape(bsz, nb, Q_BLOCK, FOX_HEADS), 0, 1)
    pos = jnp.arange(t, dtype=jnp.int32)
    pb = pos.reshape(nb, Q_BLOCK)
    out = lax.map(lambda a: _fox_attend(a[0], k, v, a[1], c, a[2], pos), (qb, cb, pb))
    return jnp.swapaxes(out, 0, 1).reshape(bsz, t, FOX_WIDTH)


def _pool_mix(u_ext, pos, w_pool, s_pool):
    bsz = u_ext.shape[0]
    t = pos.shape[0]
    cs = jnp.cumsum(u_ext.astype(jnp.float32), axis=1)
    cs = jnp.concatenate([jnp.zeros_like(cs[:, :1]), cs], axis=1)
    end = cs[:, POOL_STATE + 1:]
    means = []
    for g, w in enumerate(POOL_WINDOWS):
        sl = slice(g * POOL_GROUP_DIM, (g + 1) * POOL_GROUP_DIM)
        start = cs[:, POOL_STATE + 1 - w:POOL_STATE + 1 - w + t, sl]
        cnt = jnp.minimum(pos + 1, w).astype(jnp.float32)[None, :, None]
        means.append((end[..., sl] - start) / cnt)
    u = u_ext[:, POOL_STATE:].astype(jnp.float32)
    d = (jnp.concatenate(means, axis=-1) - u).reshape(bsz, t, POOL_GROUPS, POOL_GROUP_DIM)
    out = jnp.einsum('btgc,gce->btge', d.astype(w_pool.dtype), w_pool).reshape(bsz, t, POOL_WIDTH)
    return out * s_pool


def _mem_kv(mem, g_mem, w_xk, w_xv):
    bsz = mem.shape[0]
    mn = _rms(mem, g_mem)
    mk = (mn @ w_xk).reshape(bsz, N_MEM, XATTN_HEADS, XATTN_HEAD_DIM)
    mv = (mn @ w_xv).reshape(bsz, N_MEM, XATTN_HEADS, XATTN_HEAD_DIM)
    return mk, mv


def _cross_attn(h, mk, mv, w_xq, w_xo):
    bsz, t = h.shape[:2]
    q = (h @ w_xq).reshape(bsz, t, XATTN_HEADS, XATTN_HEAD_DIM)
    s = jnp.einsum('bthd,bmhd->bhtm', q, mk, preferred_element_type=jnp.float32) * XATTN_SCALE
    p = jax.nn.softmax(s, axis=-1)
    o = jnp.einsum('bhtm,bmhd->bthd', p.astype(mv.dtype), mv).reshape(bsz, t, XATTN_WIDTH)
    return (o @ w_xo).astype(h.dtype)


def _moe(xn, w_router, b_router, w_gu, b_gu, w_down, b_down):
    n_tok = xn.shape[0]
    n_asg = n_tok * TOP_K
    logits = (xn @ w_router).astype(jnp.float32) + b_router.astype(jnp.float32)
    top_val, top_idx = lax.top_k(logits, TOP_K)
    gates = jax.nn.softmax(top_val, axis=-1)
    e_flat = top_idx.reshape(-1)
    tok_flat = jnp.arange(n_asg, dtype=jnp.int32) // TOP_K
    order = jnp.argsort(e_flat)
    e_sorted = e_flat[order]
    tok_sorted = tok_flat[order]
    g_sorted = gates.reshape(-1)[order]
    counts = jnp.bincount(e_flat, length=N_EXPERTS)
    starts = jnp.cumsum(counts) - counts
    padded = (counts + EXPERT_BLOCK - 1) // EXPERT_BLOCK * EXPERT_BLOCK
    pad_ends = jnp.cumsum(padded)
    pad_starts = pad_ends - padded
    dest = pad_starts[e_sorted] + (jnp.arange(n_asg) - starts[e_sorted])
    n_blocks = -(-n_asg // EXPERT_BLOCK) + N_EXPERTS
    buf_tok = jnp.full((n_blocks * EXPERT_BLOCK,), n_tok, jnp.int32).at[dest].set(tok_sorted)
    buf_g = jnp.zeros((n_blocks * EXPERT_BLOCK,), jnp.float32).at[dest].set(g_sorted)
    block_e = jnp.minimum(jnp.searchsorted(pad_ends, jnp.arange(n_blocks) * EXPERT_BLOCK, side='right'), N_EXPERTS - 1)
    x_pad = jnp.concatenate([xn, jnp.zeros_like(xn[:1])], axis=0)

    def expert_block(a):
        tok_b, e = a
        xb = x_pad[tok_b]
        gu = (xb @ w_gu[e] + b_gu[e]).astype(jnp.float32)
        gate = jnp.minimum(gu[:, :D_EXPERT], SWIGLU_LIMIT)
        lin = jnp.clip(gu[:, D_EXPERT:], -SWIGLU_LIMIT, SWIGLU_LIMIT)
        hdn = (lin + 1.0) * gate * jax.nn.sigmoid(SWIGLU_ALPHA * gate)
        return hdn.astype(xn.dtype) @ w_down[e] + b_down[e]

    out_blocks = lax.map(expert_block, (buf_tok.reshape(n_blocks, EXPERT_BLOCK), block_e))
    contrib = out_blocks.reshape(-1, xn.shape[1]).astype(jnp.float32) * buf_g[:, None]
    y = jax.ops.segment_sum(contrib, buf_tok, num_segments=n_tok + 1)
    return y[:n_tok].astype(xn.dtype)


def setup_inputs(seed: int = 0) -> dict:
    key = jax.random.key(seed)
    ks = jax.random.split(key, 40)
    f32 = jnp.float32
    n_pages = PAST_LEN // PAGE_SIZE
    n_pool = (DEC_BATCH * n_pages * 5) // 4

    def nrm(k, shape, scale=1.0):
        return scale * jax.random.normal(k, shape, f32)

    def gain(k, shape):
        return 1.0 + 0.1 * jax.random.normal(k, shape, f32)

    page_table = jax.random.permutation(ks[9], n_pool)[:DEC_BATCH * n_pages].reshape(DEC_BATCH, n_pages).astype(jnp.int32)
    return {
        'x_prompt': nrm(ks[0], (BATCH, SEQ, D_MODEL)),
        'x_sample': nrm(ks[1], (DEC_BATCH, DEC_SEQ, D_MODEL)),
        'mem_prompt': nrm(ks[2], (BATCH, N_MEM, D_MODEL)),
        'cache_k': nrm(ks[3], (DEPTH, n_pool, PAGE_SIZE, FOX_HEADS, FOX_HEAD_DIM)),
        'cache_v': nrm(ks[4], (DEPTH, n_pool, PAGE_SIZE, FOX_HEADS, FOX_HEAD_DIM)),
        'cache_logf': jax.nn.log_sigmoid(3.0 + nrm(ks[5], (DEPTH, n_pool, PAGE_SIZE, FOX_HEADS))),
        'cache_mem_k': nrm(ks[6], (DEPTH, DEC_BATCH, N_MEM, XATTN_HEADS, XATTN_HEAD_DIM)),
        'cache_mem_v': nrm(ks[7], (DEPTH, DEC_BATCH, N_MEM, XATTN_HEADS, XATTN_HEAD_DIM)),
        'state_pool': nrm(ks[8], (DEPTH, DEC_BATCH, POOL_STATE, POOL_WIDTH)),
        'page_table': page_table,
        'g_mix': gain(ks[10], (DEPTH, D_MODEL)),
        'w_in': nrm(ks[11], (DEPTH, D_MODEL, IN_COLS), D_MODEL ** -0.5),
        'b_forget': 3.0 + nrm(ks[12], (DEPTH, FOX_HEADS), 0.5),
        'w_pool': nrm(ks[13], (DEPTH, POOL_GROUPS, POOL_GROUP_DIM, POOL_GROUP_DIM), POOL_GROUP_DIM ** -0.5),
        's_pool': gain(ks[14], (DEPTH, POOL_WIDTH)),
        'w_out': nrm(ks[15], (DEPTH, D_MIX, D_MODEL), D_MIX ** -0.5),
        'g_xattn': gain(ks[16], (DEPTH, D_MODEL)),
        'g_mem': gain(ks[17], (DEPTH, D_MODEL)),
        'w_xq': nrm(ks[18], (DEPTH, D_MODEL, XATTN_WIDTH), D_MODEL ** -0.5),
        'w_xk': nrm(ks[19], (DEPTH, D_MODEL, XATTN_WIDTH), D_MODEL ** -0.5),
        'w_xv': nrm(ks[20], (DEPTH, D_MODEL, XATTN_WIDTH), D_MODEL ** -0.5),
        'w_xo': nrm(ks[21], (DEPTH, XATTN_WIDTH, D_MODEL), XATTN_WIDTH ** -0.5),
        'g_ffn': gain(ks[22], (DEPTH, D_MODEL)),
        'w_router': nrm(ks[23], (DEPTH, D_MODEL, N_EXPERTS), D_MODEL ** -0.5),
        'b_router': nrm(ks[24], (DEPTH, N_EXPERTS), 0.01),
        'w_gu': nrm(ks[25], (DEPTH, N_EXPERTS, D_MODEL, 2 * D_EXPERT), D_MODEL ** -0.5),
        'b_gu': nrm(ks[26], (DEPTH, N_EXPERTS, 2 * D_EXPERT), 0.02),
        'w_down': nrm(ks[27], (DEPTH, N_EXPERTS, D_EXPERT, D_MODEL), D_EXPERT ** -0.5),
        'b_down': nrm(ks[28], (DEPTH, N_EXPERTS, D_MODEL), 0.02),
        'g_final': gain(ks[29], (D_MODEL,)),
    }


def reference(x_prompt, x_sample, mem_prompt, cache_k, cache_v, cache_logf, cache_mem_k, cache_mem_v,
              state_pool, page_table, g_mix, w_in, b_forget, w_pool, s_pool, w_out, g_xattn, g_mem,
              w_xq, w_xk, w_xv, w_xo, g_ffn, w_router, b_router, w_gu, b_gu, w_down, b_down, g_final):
    xp, xs = x_prompt, x_sample
    bp, tp = xp.shape[:2]
    bs, ts = xs.shape[:2]
    past_len = page_table.shape[1] * PAGE_SIZE
    pos_p = jnp.arange(tp, dtype=jnp.int32)
    pos_s = past_len + jnp.arange(ts, dtype=jnp.int32)
    kpos_s = jnp.arange(past_len + ts, dtype=jnp.int32)
    kp_l, vp_l, lfp_l, pp_l, mkp_l, mvp_l = [], [], [], [], [], []
    ks_l, vs_l, lfs_l, ps_l = [], [], [], []
    for l in range(DEPTH):
        q, k, v, lf, u = _split_mix(_rms(xp, g_mix[l]) @ w_in[l], b_forget[l])
        c = jnp.cumsum(lf, axis=1)
        attn = _fox_prompt(q, k, v, c)
        u_ext = jnp.concatenate([jnp.zeros((bp, POOL_STATE, POOL_WIDTH), u.dtype), u], axis=1)
        pool = _pool_mix(u_ext, pos_p, w_pool[l], s_pool[l])
        xp = xp + jnp.concatenate([attn.astype(xp.dtype), pool.astype(xp.dtype)], axis=-1) @ w_out[l]
        kp_l.append(k); vp_l.append(v); lfp_l.append(lf); pp_l.append(u_ext[:, -POOL_STATE:])
        q, k, v, lf, u = _split_mix(_rms(xs, g_mix[l]) @ w_in[l], b_forget[l])
        past_k = cache_k[l][page_table].reshape(bs, past_len, FOX_HEADS, FOX_HEAD_DIM)
        past_v = cache_v[l][page_table].reshape(bs, past_len, FOX_HEADS, FOX_HEAD_DIM)
        past_lf = cache_logf[l][page_table].reshape(bs, past_len, FOX_HEADS)
        k_all = jnp.concatenate([past_k.astype(k.dtype), k], axis=1)
        v_all = jnp.concatenate([past_v.astype(v.dtype), v], axis=1)
        c_all = jnp.cumsum(jnp.concatenate([past_lf.astype(jnp.float32), lf], axis=1), axis=1)
        attn = _fox_attend(q, k_all, v_all, c_all[:, past_len:], c_all, pos_s, kpos_s).reshape(bs, ts, FOX_WIDTH)
        u_ext = jnp.concatenate([state_pool[l].astype(u.dtype), u], axis=1)
        pool = _pool_mix(u_ext, pos_s, w_pool[l], s_pool[l])
        xs = xs + jnp.concatenate([attn.astype(xs.dtype), pool.astype(xs.dtype)], axis=-1) @ w_out[l]
        ks_l.append(k); vs_l.append(v); lfs_l.append(lf); ps_l.append(u_ext[:, -POOL_STATE:])
        mk, mv = _mem_kv(mem_prompt, g_mem[l], w_xk[l], w_xv[l])
        xp = xp + _cross_attn(_rms(xp, g_xattn[l]), mk, mv, w_xq[l], w_xo[l])
        xs = xs + _cross_attn(_rms(xs, g_xattn[l]), cache_mem_k[l].astype(xs.dtype), cache_mem_v[l].astype(xs.dtype), w_xq[l], w_xo[l])
        mkp_l.append(mk); mvp_l.append(mv)
        xp = xp + _moe(_rms(xp, g_ffn[l]).reshape(bp * tp, -1), w_router[l], b_router[l], w_gu[l], b_gu[l], w_down[l], b_down[l]).reshape(xp.shape)
        xs = xs + _moe(_rms(xs, g_ffn[l]).reshape(bs * ts, -1), w_router[l], b_router[l], w_gu[l], b_gu[l], w_down[l], b_down[l]).reshape(xs.shape)
    y_prompt = _rms(xp, g_final)
    y_sample = _rms(xs, g_final)
    return (y_prompt, y_sample,
            jnp.stack(kp_l), jnp.stack(vp_l), jnp.stack(lfp_l), jnp.stack(pp_l), jnp.stack(mkp_l), jnp.stack(mvp_l),
            jnp.stack(ks_l), jnp.stack(vs_l), jnp.stack(lfs_l), jnp.stack(ps_l))
```

```python
import functools

import jax
import jax.numpy as jnp
from jax import lax
from jax.experimental import pallas as pl
from jax.experimental.pallas import tpu as pltpu

F32 = jnp.float32
BF16 = jnp.bfloat16

HEAD_DIM = 128
PAGE_SIZE = 128
POOL_WINDOWS = (2, 4, 8, 16)
POOL_STATE = max(POOL_WINDOWS) - 1
POOL_HALO = 16
TOP_K = 4
SWIGLU_ALPHA = 1.702
SWIGLU_LIMIT = 7.0
NORM_EPS = 1e-5
LANES = 128
BF16_ROWS = 16
VMEM_LIMIT = 56 * 1024 * 1024
NEG_INF = float("-inf")
EXPERT_TILE = 256
COMBINE_TILE = 128


def _cparams(*sem):
    return pltpu.CompilerParams(dimension_semantics=sem, vmem_limit_bytes=VMEM_LIMIT)


def _rms_f32(x, g):
    return x * lax.rsqrt(jnp.mean(x * x, axis=-1, keepdims=True) + NORM_EPS) * g


def _dot(a, b):
    return jnp.dot(a, b, preferred_element_type=F32)


def _dot_nt(a, b):
    return lax.dot_general(a, b, (((1,), (1,)), ((), ())), preferred_element_type=F32)


def _pad_rows(x):
    rows = x.shape[0]
    if rows >= BF16_ROWS:
        return x
    return jnp.concatenate([x, jnp.zeros((BF16_ROWS - rows,) + x.shape[1:], x.dtype)], axis=0)


def _split3(x):
    hi = x.astype(BF16)
    r = x - hi.astype(F32)
    mid = r.astype(BF16)
    lo = (r - mid.astype(F32)).astype(BF16)
    return hi, mid, lo


def _lane_cumsum(x, seg):
    n = x.shape[-1]
    r = lax.broadcasted_iota(jnp.int32, (n, n), 0)
    c = lax.broadcasted_iota(jnp.int32, (n, n), 1)
    keep = r <= c
    if seg < n:
        keep = jnp.logical_and(keep, (r // seg) == (c // seg))
    tri = jnp.where(keep, 1.0, 0.0).astype(BF16)
    rows = x.shape[0]
    hi, mid, lo = _split3(_pad_rows(x))
    return (_dot(hi, tri) + _dot(mid, tri) + _dot(lo, tri))[:rows]


def _log_sigmoid(z):
    return jnp.minimum(z, 0.0) - jnp.log1p(jnp.exp(-jnp.abs(z)))


def _in_proj_kernel(x_ref, g_ref, w_ref, wf_ref, bf_ref,
                    q_ref, k_ref, v_ref, u_ref, kb_ref, vb_ref, lf_ref, ct_ref,
                    xn_scr, carry_scr, *, tiles_per_seq, seg, heads):
    i = pl.program_id(0)
    j = pl.program_id(1)

    @pl.when(j == 0)
    def _():
        xn = _rms_f32(x_ref[...], g_ref[...]).astype(BF16)
        xn_scr[...] = xn
        lf = _log_sigmoid(_dot(xn, wf_ref[...]) + bf_ref[...])
        lf_ref[...] = lf[:, :heads]
        lft = lf.T[:8]
        c = _lane_cumsum(lft, seg)

        @pl.when(i % tiles_per_seq == 0)
        def _():
            carry_scr[...] = jnp.zeros_like(carry_scr)

        c = c + carry_scr[:, :1]
        ct_ref[0] = c
        carry_scr[...] = jnp.broadcast_to(c[:, -1:], carry_scr.shape)

    y = _dot(xn_scr[...], w_ref[...])

    @pl.when(j == 0)
    def _():
        q_ref[...] = y

    @pl.when(j == 1)
    def _():
        k_ref[...] = y
        kb_ref[...] = y.astype(BF16)

    @pl.when(j == 2)
    def _():
        v_ref[...] = y
        vb_ref[...] = y.astype(BF16)

    @pl.when(j == 3)
    def _():
        u_ref[...] = y


def _in_proj(x, g, w4, wf, bfp, *, seq_len, heads, tm):
    n, d = x.shape
    width = w4.shape[1] // 4
    if seq_len >= tm:
        tiles_per_seq, seg = seq_len // tm, tm
    else:
        tiles_per_seq, seg = 1, seq_len
    n_tiles = n // tm
    row = lambda i, j: (i, 0)
    big = jax.ShapeDtypeStruct((n, width), F32)
    bigb = jax.ShapeDtypeStruct((n, width), BF16)
    return pl.pallas_call(
        functools.partial(_in_proj_kernel, tiles_per_seq=tiles_per_seq, seg=seg, heads=heads),
        grid=(n_tiles, 4),
        in_specs=[
            pl.BlockSpec((tm, d), row),
            pl.BlockSpec((1, d), lambda i, j: (0, 0)),
            pl.BlockSpec((d, width), lambda i, j: (0, j)),
            pl.BlockSpec((d, LANES), lambda i, j: (0, 0)),
            pl.BlockSpec((1, LANES), lambda i, j: (0, 0)),
        ],
        out_specs=[
            pl.BlockSpec((tm, width), row), pl.BlockSpec((tm, width), row),
            pl.BlockSpec((tm, width), row), pl.BlockSpec((tm, width), row),
            pl.BlockSpec((tm, width), row), pl.BlockSpec((tm, width), row),
            pl.BlockSpec((tm, heads), row),
            pl.BlockSpec((1, 8, tm), lambda i, j: (i, 0, 0)),
        ],
        out_shape=[big, big, big, big, bigb, bigb,
                   jax.ShapeDtypeStruct((n, heads), F32),
                   jax.ShapeDtypeStruct((n_tiles, 8, tm), F32)],
        scratch_shapes=[pltpu.VMEM((tm, d), BF16), pltpu.VMEM((8, LANES), F32)],
        compiler_params=_cparams("arbitrary", "arbitrary"),
        name="in_proj",
    )(x, g, w4, wf, bfp)


def _fox_flash_kernel(q_ref, k_ref, v_ref, ct_ref, o_ref, m_scr, l_scr, acc_scr, *, scale):
    h = pl.program_id(1)
    qi = pl.program_id(2)
    ki = pl.program_id(3)

    @pl.when(ki == 0)
    def _():
        m_scr[...] = jnp.full_like(m_scr, NEG_INF)
        l_scr[...] = jnp.zeros_like(l_scr)
        acc_scr[...] = jnp.zeros_like(acc_scr)

    def step(diagonal):
        q = (q_ref[...] * scale).astype(BF16)
        s = _dot_nt(q, k_ref[...]) - ct_ref[0, pl.ds(h, 1), :]
        if diagonal:
            r = lax.broadcasted_iota(jnp.int32, s.shape, 0)
            c = lax.broadcasted_iota(jnp.int32, s.shape, 1)
            s = jnp.where(c <= r, s, NEG_INF)
        m_prev = m_scr[...]
        m_new = jnp.maximum(m_prev, jnp.max(s, axis=-1, keepdims=True))
        alpha = jnp.exp(m_prev - m_new)
        p = jnp.exp(s - m_new[:, :1])
        l_scr[...] = alpha * l_scr[...] + jnp.sum(p, axis=-1, keepdims=True)
        acc_scr[...] = alpha * acc_scr[...] + _dot(p.astype(BF16), v_ref[...])
        m_scr[...] = m_new

    @pl.when(ki < qi)
    def _():
        step(False)

    @pl.when(ki == qi)
    def _():
        step(True)
        o_ref[...] = (acc_scr[...] / l_scr[...]).astype(o_ref.dtype)


def _fox_prompt(q, kb, vb, ct, *, batch, seq_len, heads, tq):
    n, width = q.shape
    nq = seq_len // tq
    kv = lambda b, h, qi, ki: (b * nq + jnp.minimum(ki, qi), h)
    return pl.pallas_call(
        functools.partial(_fox_flash_kernel, scale=HEAD_DIM ** -0.5),
        grid=(batch, heads, nq, nq),
        in_specs=[
            pl.BlockSpec((tq, HEAD_DIM), lambda b, h, qi, ki: (b * nq + qi, h)),
            pl.BlockSpec((tq, HEAD_DIM), kv),
            pl.BlockSpec((tq, HEAD_DIM), kv),
            pl.BlockSpec((1, 8, tq), lambda b, h, qi, ki: (b * nq + jnp.minimum(ki, qi), 0, 0)),
        ],
        out_specs=pl.BlockSpec((tq, HEAD_DIM), lambda b, h, qi, ki: (b * nq + qi, h)),
        out_shape=jax.ShapeDtypeStruct((n, width), BF16),
        scratch_shapes=[pltpu.VMEM((tq, LANES), F32), pltpu.VMEM((tq, LANES), F32),
                        pltpu.VMEM((tq, HEAD_DIM), F32)],
        compiler_params=_cparams("arbitrary", "arbitrary", "arbitrary", "arbitrary"),
        name="fox_prompt",
    )(q, kb, vb, ct)


def _fox_decode_kernel(pt_ref, q_ref, kc_ref, vc_ref, lft_ref, kn_ref, vn_ref, ctn_ref, o_ref,
                       m_scr, l_scr, acc_scr, carry_scr, *, scale, heads, n_new):
    del pt_ref
    p = pl.program_id(1)
    last = pl.num_programs(1) - 1

    @pl.when(p == 0)
    def _():
        m_scr[...] = jnp.full_like(m_scr, NEG_INF)
        l_scr[...] = jnp.zeros_like(l_scr)
        acc_scr[...] = jnp.zeros_like(acc_scr)
        carry_scr[...] = jnp.zeros_like(carry_scr)

    def attend(h, kh, vh, bias_row, mask):
        qh = q_ref[:, h * HEAD_DIM:(h + 1) * HEAD_DIM] * scale
        s = _dot_nt(_pad_rows(qh).astype(BF16), kh.astype(BF16))[:n_new] - bias_row
        if mask is not None:
            s = jnp.where(mask, s, NEG_INF)
        m_prev = m_scr[h]
        m_new = jnp.maximum(m_prev, jnp.max(s, axis=-1, keepdims=True))
        alpha = jnp.exp(m_prev - m_new)
        pr = jnp.exp(s - m_new)
        l_scr[h] = alpha * l_scr[h] + jnp.sum(pr, axis=-1, keepdims=True)
        acc_scr[h] = alpha * acc_scr[h] + _dot(_pad_rows(pr).astype(BF16), vh.astype(BF16))[:n_new]
        m_scr[h] = m_new

    c_page = _lane_cumsum(lft_ref[0], PAGE_SIZE) + carry_scr[...]
    carry_scr[...] = jnp.broadcast_to(c_page[:, -1:], carry_scr.shape)
    for h in range(heads):
        attend(h, kc_ref[0, pl.ds(h, PAGE_SIZE, stride=heads), :],
               vc_ref[0, pl.ds(h, PAGE_SIZE, stride=heads), :], c_page[h:h + 1, :], None)

    @pl.when(p == last)
    def _():
        r = lax.broadcasted_iota(jnp.int32, (n_new, PAGE_SIZE), 0)
        c = lax.broadcasted_iota(jnp.int32, (n_new, PAGE_SIZE), 1)
        mask = c <= r
        pad = jnp.zeros((PAGE_SIZE - n_new, HEAD_DIM), F32)
        c_new = jnp.concatenate(
            [ctn_ref[0], jnp.zeros((8, PAGE_SIZE - n_new), F32)], axis=1) + carry_scr[...]
        for h in range(heads):
            kh = jnp.concatenate([kn_ref[:, h * HEAD_DIM:(h + 1) * HEAD_DIM], pad], axis=0)
            vh = jnp.concatenate([vn_ref[:, h * HEAD_DIM:(h + 1) * HEAD_DIM], pad], axis=0)
            attend(h, kh, vh, c_new[h:h + 1, :], mask)
            o_ref[:, h * HEAD_DIM:(h + 1) * HEAD_DIM] = acc_scr[h] / l_scr[h]


def _fox_decode(q, k_new, v_new, ct_new, kc, vc, lft, page_table, *, heads, n_new):
    n, width = q.shape
    n_seq, n_pages = page_table.shape
    row = lambda b, p, pt: (b, 0)
    page = lambda b, p, pt: (pt[b * n_pages + p], 0, 0)
    grid_spec = pltpu.PrefetchScalarGridSpec(
        num_scalar_prefetch=1,
        grid=(n_seq, n_pages),
        in_specs=[
            pl.BlockSpec((n_new, width), row),
            pl.BlockSpec((1, PAGE_SIZE * heads, HEAD_DIM), page),
            pl.BlockSpec((1, PAGE_SIZE * heads, HEAD_DIM), page),
            pl.BlockSpec((1, 8, PAGE_SIZE), page),
            pl.BlockSpec((n_new, width), row),
            pl.BlockSpec((n_new, width), row),
            pl.BlockSpec((1, 8, n_new), lambda b, p, pt: (b, 0, 0)),
        ],
        out_specs=pl.BlockSpec((n_new, width), row),
        scratch_shapes=[pltpu.VMEM((heads, n_new, LANES), F32), pltpu.VMEM((heads, n_new, LANES), F32),
                        pltpu.VMEM((heads, n_new, HEAD_DIM), F32), pltpu.VMEM((8, LANES), F32)],
    )
    return pl.pallas_call(
        functools.partial(_fox_decode_kernel, scale=HEAD_DIM ** -0.5, heads=heads, n_new=n_new),
        grid_spec=grid_spec,
        out_shape=jax.ShapeDtypeStruct((n, width), F32),
        compiler_params=_cparams("arbitrary", "arbitrary"),
        name="fox_decode",
    )(page_table.reshape(-1), q, kc, vc, lft, k_new, v_new, ct_new)


def _pool_prompt_kernel(u_ref, halo_ref, w_ref, s_ref, o_ref, *, tiles_per_seq, gd):
    tm = u_ref.shape[0]
    tile_in_seq = pl.program_id(0) % tiles_per_seq
    pos = tile_in_seq * tm + lax.broadcasted_iota(jnp.int32, (tm, 1), 0)
    for g, w in enumerate(POOL_WINDOWS):
        sl = slice(g * gd, (g + 1) * gd)
        u = u_ref[:, sl]
        halo = jnp.where(tile_in_seq > 0, halo_ref[:, sl], 0.0)
        acc = jnp.concatenate([halo, u], axis=0)
        k = 1
        while k < w:
            acc = acc + pltpu.roll(acc, k, axis=0)
            k *= 2
        cnt = jnp.minimum(pos + 1, w).astype(F32)
        d = acc[POOL_HALO:] / cnt - u
        o_ref[:, sl] = (_dot(d.astype(BF16), w_ref[g]) * s_ref[:, sl]).astype(o_ref.dtype)


def _pool_prompt(u, w_pool, s_pool, *, seq_len, tm):
    n, pw = u.shape
    gd = pw // len(POOL_WINDOWS)
    halo_blocks = tm // POOL_HALO
    return pl.pallas_call(
        functools.partial(_pool_prompt_kernel, tiles_per_seq=seq_len // tm, gd=gd),
        grid=(n // tm,),
        in_specs=[
            pl.BlockSpec((tm, pw), lambda i: (i, 0)),
            pl.BlockSpec((POOL_HALO, pw), lambda i: (jnp.maximum(i * halo_blocks - 1, 0), 0)),
            pl.BlockSpec(w_pool.shape, lambda i: (0, 0, 0)),
            pl.BlockSpec((1, pw), lambda i: (0, 0)),
        ],
        out_specs=pl.BlockSpec((tm, pw), lambda i: (i, 0)),
        out_shape=jax.ShapeDtypeStruct((n, pw), BF16),
        compiler_params=_cparams("arbitrary"),
        name="pool_prompt",
    )(u, u, w_pool, s_pool)


def _pool_sample_kernel(u_ref, st_ref, w_ref, s_ref, o_ref, *, gd, past_len):
    n_new = u_ref.shape[0]

    def ext(j, sl):
        return st_ref[j, :, sl] if j < POOL_STATE else u_ref[j - POOL_STATE, :, sl]

    for g, w in enumerate(POOL_WINDOWS):
        sl = slice(g * gd, (g + 1) * gd)
        for t in range(n_new):
            cur = ext(POOL_STATE + t, sl)
            acc = cur
            for j in range(1, w):
                acc = acc + ext(POOL_STATE + t - j, sl)
            d = acc / float(min(past_len + t + 1, w)) - cur
            o_ref[t, :, sl] = (_dot(d.astype(BF16), w_ref[g]) * s_ref[:, sl]).astype(o_ref.dtype)


def _pool_sample(u_t, st_t, w_pool, s_pool, *, past_len):
    n_new, n_seq, pw = u_t.shape
    gd = pw // len(POOL_WINDOWS)
    return pl.pallas_call(
        functools.partial(_pool_sample_kernel, gd=gd, past_len=past_len),
        out_shape=jax.ShapeDtypeStruct((n_new, n_seq, pw), BF16),
        compiler_params=pltpu.CompilerParams(vmem_limit_bytes=VMEM_LIMIT),
        name="pool_sample",
    )(u_t, st_t, w_pool, s_pool)


def _mix_out_kernel(x_ref, a_ref, p_ref, wa_ref, wp_ref, g_ref, wq_ref, x1_ref, qx_ref):
    x1 = x_ref[...] + _dot(a_ref[...].astype(BF16), wa_ref[...]) + _dot(p_ref[...], wp_ref[...])
    x1_ref[...] = x1
    qx_ref[...] = _dot(_rms_f32(x1, g_ref[...]).astype(BF16), wq_ref[...])


def _mix_out(x, attn, pool, wa, wp, g, wq, *, tm):
    n, d = x.shape
    xw = wq.shape[1]
    row = lambda i: (i, 0)
    fixed = lambda i: (0, 0)
    return pl.pallas_call(
        _mix_out_kernel,
        grid=(n // tm,),
        in_specs=[
            pl.BlockSpec((tm, d), row), pl.BlockSpec((tm, attn.shape[1]), row),
            pl.BlockSpec((tm, pool.shape[1]), row),
            pl.BlockSpec(wa.shape, fixed), pl.BlockSpec(wp.shape, fixed),
            pl.BlockSpec((1, d), fixed), pl.BlockSpec(wq.shape, fixed),
        ],
        out_specs=[pl.BlockSpec((tm, d), row), pl.BlockSpec((tm, xw), row)],
        out_shape=[jax.ShapeDtypeStruct((n, d), F32), jax.ShapeDtypeStruct((n, xw), F32)],
        compiler_params=_cparams("arbitrary"),
        name="mix_out",
    )(x, attn, pool, wa, wp, g, wq)


def _mem_kv_kernel(m_ref, g_ref, w_ref, k_ref, v_ref):
    y = _dot(_rms_f32(m_ref[...], g_ref[...]).astype(BF16), w_ref[...])
    xw = k_ref.shape[1]
    k_ref[...] = y[:, :xw]
    v_ref[...] = y[:, xw:]


def _mem_kv(mem, g, wkv, *, tm):
    n, d = mem.shape
    xw = wkv.shape[1] // 2
    out = jax.ShapeDtypeStruct((n, xw), F32)
    return pl.pallas_call(
        _mem_kv_kernel,
        grid=(n // tm,),
        in_specs=[pl.BlockSpec((tm, d), lambda i: (i, 0)), pl.BlockSpec((1, d), lambda i: (0, 0)),
                  pl.BlockSpec(wkv.shape, lambda i: (0, 0))],
        out_specs=[pl.BlockSpec((tm, xw), lambda i: (i, 0))] * 2,
        out_shape=[out, out],
        compiler_params=_cparams("arbitrary"),
        name="mem_kv",
    )(mem, g, wkv)


def _xattn_kernel(q_ref, k_ref, v_ref, o_ref, *, scale, heads):
    tm = q_ref.shape[0]
    for h in range(heads):
        sl = slice(h * HEAD_DIM, (h + 1) * HEAD_DIM)
        q = _pad_rows(q_ref[:, sl] * scale).astype(BF16)
        s = _dot_nt(q, k_ref[:, sl].astype(BF16))
        p = jnp.exp(s - jnp.max(s, axis=-1, keepdims=True))
        o = _dot(p.astype(BF16), v_ref[:, sl].astype(BF16)) / jnp.sum(p, axis=-1, keepdims=True)
        o_ref[:, sl] = o[:tm]


def _xattn(q, mk, mv, *, rows_per_mem, n_mem, tm):
    n, xw = q.shape
    tiles_per_mem = rows_per_mem // tm
    mem = lambda i: (i // tiles_per_mem, 0)
    return pl.pallas_call(
        functools.partial(_xattn_kernel, scale=HEAD_DIM ** -0.5, heads=xw // HEAD_DIM),
        grid=(n // tm,),
        in_specs=[pl.BlockSpec((tm, xw), lambda i: (i, 0)),
                  pl.BlockSpec((n_mem, xw), mem), pl.BlockSpec((n_mem, xw), mem)],
        out_specs=pl.BlockSpec((tm, xw), lambda i: (i, 0)),
        out_shape=jax.ShapeDtypeStruct((n, xw), F32),
        compiler_params=_cparams("arbitrary"),
        name="xattn",
    )(q, mk, mv)


def _xattn_out_kernel(x_ref, o_ref, wo_ref, g_ref, wr_ref, br_ref, x2_ref, xn_ref, lg_ref):
    x2 = x_ref[...] + _dot(o_ref[...].astype(BF16), wo_ref[...])
    x2_ref[...] = x2
    xn = _rms_f32(x2, g_ref[...])
    xn_ref[...] = xn
    lg_ref[...] = _dot(xn.astype(BF16), wr_ref[...]) + br_ref[...]


def _xattn_out(x, o, wo, g, wr, br, *, tm):
    n, d = x.shape
    row = lambda i: (i, 0)
    fixed = lambda i: (0, 0)
    return pl.pallas_call(
        _xattn_out_kernel,
        grid=(n // tm,),
        in_specs=[pl.BlockSpec((tm, d), row), pl.BlockSpec((tm, o.shape[1]), row),
                  pl.BlockSpec(wo.shape, fixed), pl.BlockSpec((1, d), fixed),
                  pl.BlockSpec(wr.shape, fixed), pl.BlockSpec((1, LANES), fixed)],
        out_specs=[pl.BlockSpec((tm, d), row), pl.BlockSpec((tm, d), row),
                   pl.BlockSpec((tm, LANES), row)],
        out_shape=[jax.ShapeDtypeStruct((n, d), F32), jax.ShapeDtypeStruct((n, d), F32),
                   jax.ShapeDtypeStruct((n, LANES), F32)],
        compiler_params=_cparams("arbitrary"),
        name="xattn_out",
    )(x, o, wo, g, wr, br)


def _gather_rows_kernel(nblk_ref, idx_ref, x_hbm, o_ref, sem):
    tg = o_ref.shape[0]

    def row_copy(r):
        return pltpu.make_async_copy(x_hbm.at[pl.ds(idx_ref[0, r], 1)], o_ref.at[pl.ds(r, 1)], sem)

    used = pl.program_id(0) < nblk_ref[0]

    @pl.when(used)
    def _():
        lax.fori_loop(0, tg, lambda r, c: (row_copy(r).start(), c)[1], 0)
        lax.fori_loop(0, tg, lambda r, c: (row_copy(r).wait(), c)[1], 0)

    @pl.when(jnp.logical_not(used))
    def _():
        o_ref[...] = jnp.zeros_like(o_ref)


def _gather_rows(x, idx, nblk, *, tg):
    n_tiles = idx.shape[0]
    d = x.shape[1]
    grid_spec = pltpu.PrefetchScalarGridSpec(
        num_scalar_prefetch=1,
        grid=(n_tiles,),
        in_specs=[pl.BlockSpec((None, 1, tg), lambda i, nb: (jnp.minimum(i, nb[0] - 1), 0, 0),
                               memory_space=pltpu.SMEM),
                  pl.BlockSpec(memory_space=pl.ANY)],
        out_specs=pl.BlockSpec((tg, d), lambda i, nb: (i, 0)),
        scratch_shapes=[pltpu.SemaphoreType.DMA(())],
    )
    return pl.pallas_call(
        _gather_rows_kernel,
        grid_spec=grid_spec,
        out_shape=jax.ShapeDtypeStruct((n_tiles * tg, d), x.dtype),
        compiler_params=_cparams("arbitrary"),
        name="moe_gather",
    )(nblk, idx, x)


def _gate_up_kernel(be_ref, nblk_ref, x_ref, wg_ref, wl_ref, bg_ref, bl_ref, h_ref):
    del be_ref

    used = pl.program_id(1) < nblk_ref[0]

    @pl.when(jnp.logical_not(used))
    def _():
        h_ref[...] = jnp.zeros_like(h_ref)

    @pl.when(used)
    def _():
        x = x_ref[...].astype(BF16)
        gate = jnp.minimum(_dot(x, wg_ref[...].astype(BF16)) + bg_ref[...], SWIGLU_LIMIT)
        lin = jnp.clip(_dot(x, wl_ref[...].astype(BF16)) + bl_ref[...], -SWIGLU_LIMIT, SWIGLU_LIMIT)
        h_ref[...] = ((lin + 1.0) * gate * jax.nn.sigmoid(SWIGLU_ALPHA * gate)).astype(h_ref.dtype)


def _gate_up(xs, w_gu, b_gu, blk_e, nblk, *, tm, tn):
    rows, d = xs.shape
    n_exp, _, de2 = w_gu.shape
    de = de2 // 2
    nj = de // tn
    used = lambda j, r, be, nb: (jnp.minimum(r, nb[0] - 1), 0)
    exp_col = lambda off: (lambda j, r, be, nb: (be[jnp.minimum(r, nb[0] - 1)], 0, j + off))
    grid_spec = pltpu.PrefetchScalarGridSpec(
        num_scalar_prefetch=2,
        grid=(nj, rows // tm),
        in_specs=[pl.BlockSpec((tm, d), used),
                  pl.BlockSpec((None, d, tn), exp_col(0)), pl.BlockSpec((None, d, tn), exp_col(nj)),
                  pl.BlockSpec((None, 1, tn), exp_col(0)), pl.BlockSpec((None, 1, tn), exp_col(nj))],
        out_specs=pl.BlockSpec((tm, tn), lambda j, r, be, nb: (r, j)),
    )
    b3 = b_gu.reshape(n_exp, 1, de2)
    return pl.pallas_call(
        _gate_up_kernel,
        grid_spec=grid_spec,
        out_shape=jax.ShapeDtypeStruct((rows, de), BF16),
        compiler_params=_cparams("arbitrary", "arbitrary"),
        name="moe_gate_up",
    )(blk_e, nblk, xs, w_gu, w_gu, b3, b3)


def _down_kernel(be_ref, nblk_ref, h_ref, w_ref, b_ref, g_ref, o_ref):
    del be_ref

    used = pl.program_id(1) < nblk_ref[0]

    @pl.when(jnp.logical_not(used))
    def _():
        o_ref[...] = jnp.zeros_like(o_ref)

    @pl.when(used)
    def _():
        o_ref[...] = (_dot(h_ref[...], w_ref[...].astype(BF16)) + b_ref[...]) * g_ref[...]


def _down(hdn, w_down, b_down, row_gate, blk_e, nblk, *, tm, tn):
    rows, de = hdn.shape
    n_exp, _, d = w_down.shape
    used = lambda j, r, be, nb: (jnp.minimum(r, nb[0] - 1), 0)
    exp_col = lambda j, r, be, nb: (be[jnp.minimum(r, nb[0] - 1)], 0, j)
    grid_spec = pltpu.PrefetchScalarGridSpec(
        num_scalar_prefetch=2,
        grid=(d // tn, rows // tm),
        in_specs=[pl.BlockSpec((tm, de), used),
                  pl.BlockSpec((None, de, tn), exp_col), pl.BlockSpec((None, 1, tn), exp_col),
                  pl.BlockSpec((tm, 1), used)],
        out_specs=pl.BlockSpec((tm, tn), lambda j, r, be, nb: (r, j)),
    )
    return pl.pallas_call(
        _down_kernel,
        grid_spec=grid_spec,
        out_shape=jax.ShapeDtypeStruct((rows, d), F32),
        compiler_params=_cparams("arbitrary", "arbitrary"),
        name="moe_down",
    )(blk_e, nblk, hdn, w_down, b_down.reshape(n_exp, 1, d), row_gate)


def _combine_kernel(pos_ref, x_ref, rows_hbm, g_ref, y_ref, buf, sem):
    tc = x_ref.shape[0]

    def row_copy(i):
        k, r = i // tc, i % tc
        return pltpu.make_async_copy(rows_hbm.at[pl.ds(pos_ref[0, i], 1)], buf.at[k, pl.ds(r, 1)], sem)

    lax.fori_loop(0, TOP_K * tc, lambda i, c: (row_copy(i).start(), c)[1], 0)
    lax.fori_loop(0, TOP_K * tc, lambda i, c: (row_copy(i).wait(), c)[1], 0)
    x3 = x_ref[...] + ((buf[0] + buf[1]) + (buf[2] + buf[3]))
    y_ref[...] = _rms_f32(x3, g_ref[...])


def _combine(x, rows, pos, g, *, tc):
    n, d = x.shape
    return pl.pallas_call(
        _combine_kernel,
        grid=(n // tc,),
        in_specs=[pl.BlockSpec((None, 1, TOP_K * tc), lambda i: (i, 0, 0), memory_space=pltpu.SMEM),
                  pl.BlockSpec((tc, d), lambda i: (i, 0)),
                  pl.BlockSpec(memory_space=pl.ANY),
                  pl.BlockSpec((1, d), lambda i: (0, 0))],
        out_specs=pl.BlockSpec((tc, d), lambda i: (i, 0)),
        out_shape=jax.ShapeDtypeStruct((n, d), F32),
        scratch_shapes=[pltpu.VMEM((TOP_K, tc, d), F32), pltpu.SemaphoreType.DMA(())],
        compiler_params=_cparams("arbitrary"),
        name="moe_combine",
    )(pos, x, rows, g)


def _route(logits, n_exp, tm):
    n_tok = logits.shape[0]
    n_asg = n_tok * TOP_K
    top_val, top_idx = lax.top_k(logits, TOP_K)
    gates = jax.nn.softmax(top_val, axis=-1).reshape(-1)
    e_flat = top_idx.reshape(-1).astype(jnp.int32)
    onehot = (e_flat[:, None] == jnp.arange(n_exp, dtype=jnp.int32)[None, :]).astype(jnp.int32)
    csum = jnp.cumsum(onehot, axis=0)
    rank = jnp.sum(onehot * csum, axis=1) - 1
    counts = csum[-1]
    padded = (counts + tm - 1) // tm * tm
    pad_ends = jnp.cumsum(padded)
    dest = (pad_ends - padded)[e_flat] + rank
    n_blocks = -(-n_asg // tm) + n_exp
    tok = jnp.arange(n_asg, dtype=jnp.int32) // TOP_K
    row_tok = jnp.zeros((n_blocks * tm,), jnp.int32).at[dest].set(tok)
    row_gate = jnp.zeros((n_blocks * tm,), F32).at[dest].set(gates)
    blk_e = jnp.minimum(jnp.searchsorted(pad_ends, jnp.arange(n_blocks, dtype=jnp.int32) * tm, side="right"),
                        n_exp - 1).astype(jnp.int32)
    nblk = (pad_ends[-1] // tm).astype(jnp.int32).reshape(1)
    return row_tok, row_gate, dest.astype(jnp.int32), blk_e, nblk


def _moe_final(x2_groups, xn_groups, logit_groups, w_gu, b_gu, w_down, b_down, g_final):
    n_exp = w_gu.shape[0]
    d = x2_groups[0].shape[1]
    tm = min(EXPERT_TILE, TOP_K * sum(x.shape[0] for x in x2_groups))
    xn = jnp.concatenate(xn_groups, axis=0)
    logits = jnp.concatenate(logit_groups, axis=0)[:, :n_exp]
    row_tok, row_gate, dest, blk_e, nblk = _route(logits, n_exp, tm)
    n_tiles = row_tok.shape[0] // tm
    xs = _gather_rows(xn, row_tok.reshape(n_tiles, 1, tm), nblk, tg=tm)
    hdn = _gate_up(xs, w_gu, b_gu, blk_e, nblk, tm=tm, tn=min(1024, w_gu.shape[2] // 2))
    rows = _down(hdn, w_down, b_down, row_gate.reshape(-1, 1), blk_e, nblk, tm=tm, tn=min(1024, d))
    outs, start = [], 0
    for x2 in x2_groups:
        n = x2.shape[0]
        tc = min(COMBINE_TILE, n)
        pos = dest[start * TOP_K:(start + n) * TOP_K].reshape(n // tc, tc, TOP_K)
        pos = pos.transpose(0, 2, 1).reshape(n // tc, 1, TOP_K * tc)
        outs.append(_combine(x2, rows, pos, g_final, tc=tc))
        start += n
    return outs


def _layer_common(x, attn, pool, wa, wp, g_x, wq, mk, mv, rows_per_mem, xattn_tile,
                  wo, g_f, wr, br):
    n = x.shape[0]
    x1, qx = _mix_out(x, attn, pool, wa, wp, g_x, wq, tm=min(256, n))
    o = _xattn(qx, mk, mv, rows_per_mem=rows_per_mem, n_mem=mk.shape[0] * rows_per_mem // n,
               tm=xattn_tile)
    return _xattn_out(x1, o, wo, g_f, wr, br, tm=min(256, n))


def kernel(x_prompt, x_sample, mem_prompt, cache_k, cache_v, cache_logf, cache_mem_k, cache_mem_v,
           state_pool, page_table, g_mix, w_in, b_forget, w_pool, s_pool, w_out, g_xattn, g_mem,
           w_xq, w_xk, w_xv, w_xo, g_ffn, w_router, b_router, w_gu, b_gu, w_down, b_down, g_final):
    depth = w_in.shape[0]
    bp, tp, d = x_prompt.shape
    bs, ts, _ = x_sample.shape
    n_pool, page, heads, hd = cache_k.shape[1:]
    fw = heads * hd
    pw = state_pool.shape[-1]
    n_mem = mem_prompt.shape[1]
    xw = w_xq.shape[-1]
    n_exp = w_router.shape[-1]
    past_len = page_table.shape[1] * page
    assert hd == HEAD_DIM and page == PAGE_SIZE and heads == 8 and pw == fw
    assert w_in.shape[-1] == 3 * fw + heads + pw and state_pool.shape[2] == POOL_STATE
    assert depth == 1, "the experts of all groups are evaluated together after the last layer"

    xp = x_prompt.reshape(bp * tp, d)
    xs = x_sample.reshape(bs * ts, d)
    row = lambda a: a.reshape(1, -1)
    outs = {k: [] for k in ("kp", "vp", "lfp", "pp", "mkp", "mvp", "ks", "vs", "lfs", "ps")}
    for l in range(depth):
        w = w_in[l]
        w4 = jnp.concatenate([w[:, :3 * fw], w[:, 3 * fw + heads:]], axis=1).astype(BF16)
        wf = jnp.pad(w[:, 3 * fw:3 * fw + heads], ((0, 0), (0, LANES - heads))).astype(BF16)
        bfp = jnp.pad(b_forget[l], (0, LANES - heads)).reshape(1, LANES)
        wpool = w_pool[l].astype(BF16)
        spool = row(s_pool[l])
        wa = w_out[l, :fw].astype(BF16)
        wp = w_out[l, fw:].astype(BF16)
        wq = w_xq[l].astype(BF16)
        wkv = jnp.concatenate([w_xk[l], w_xv[l]], axis=1).astype(BF16)
        wo = w_xo[l].astype(BF16)
        wr = jnp.pad(w_router[l], ((0, 0), (0, LANES - n_exp))).astype(BF16)
        br = jnp.pad(b_router[l], (0, LANES - n_exp)).reshape(1, LANES)

        tm = min(512, tp)
        q, k, v, u, kb, vb, lf, ct = _in_proj(xp, row(g_mix[l]), w4, wf, bfp, seq_len=tp, heads=heads, tm=tm)
        attn = _fox_prompt(q, kb, vb, ct, batch=bp, seq_len=tp, heads=heads, tq=tm)
        pool = _pool_prompt(u, wpool, spool, seq_len=tp, tm=tm)
        outs["kp"].append(k.reshape(bp, tp, heads, hd))
        outs["vp"].append(v.reshape(bp, tp, heads, hd))
        outs["lfp"].append(lf.reshape(bp, tp, heads))
        outs["pp"].append(u.reshape(bp, tp, pw)[:, tp - POOL_STATE:])
        mk, mv = _mem_kv(mem_prompt.reshape(bp * n_mem, d), row(g_mem[l]), wkv, tm=min(256, bp * n_mem))
        outs["mkp"].append(mk.reshape(bp, n_mem, xw // HEAD_DIM, HEAD_DIM))
        outs["mvp"].append(mv.reshape(bp, n_mem, xw // HEAD_DIM, HEAD_DIM))
        x2p, xnp_, lgp = _layer_common(xp, attn, pool, wa, wp, row(g_xattn[l]), wq, mk, mv, tp, tm,
                                       wo, row(g_ffn[l]), wr, br)

        ns = bs * ts
        tms = min(512, ns)
        q, k, v, u, _, _, lf, ct = _in_proj(xs, row(g_mix[l]), w4, wf, bfp, seq_len=ts, heads=heads, tm=tms)
        ct_new = ct.reshape(ns // tms, 8, tms // ts, ts).transpose(0, 2, 1, 3).reshape(bs, 8, ts)
        attn = _fox_decode(q, k, v, ct_new,
                           cache_k[l].reshape(n_pool, page * heads, hd),
                           cache_v[l].reshape(n_pool, page * heads, hd),
                           cache_logf[l].transpose(0, 2, 1), page_table, heads=heads, n_new=ts)
        u3 = u.reshape(bs, ts, pw)
        pool = _pool_sample(u3.transpose(1, 0, 2), state_pool[l].transpose(1, 0, 2), wpool, spool,
                            past_len=past_len)
        pool = pool.transpose(1, 0, 2).reshape(ns, pw)
        outs["ks"].append(k.reshape(bs, ts, heads, hd))
        outs["vs"].append(v.reshape(bs, ts, heads, hd))
        outs["lfs"].append(lf.reshape(bs, ts, heads))
        outs["ps"].append(jnp.concatenate([state_pool[l], u3], axis=1)[:, ts:])
        mks = cache_mem_k[l].reshape(bs * n_mem, xw)
        mvs = cache_mem_v[l].reshape(bs * n_mem, xw)
        x2s, xns, lgs = _layer_common(xs, attn, pool, wa, wp, row(g_xattn[l]), wq, mks, mvs, ts, ts,
                                      wo, row(g_ffn[l]), wr, br)

        yp, ys = _moe_final([x2p, x2s], [xnp_, xns], [lgp, lgs], w_gu[l], b_gu[l], w_down[l], b_down[l],
                            row(g_final))

    st = lambda name: jnp.stack(outs[name])
    return (yp.reshape(bp, tp, d), ys.reshape(bs, ts, d),
            st("kp"), st("vp"), st("lfp"), st("pp"), st("mkp"), st("mvp"),
            st("ks"), st("vs"), st("lfs"), st("ps"))
```

```python
import functools

import jax
import jax.numpy as jnp
from jax import lax
from jax.experimental import pallas as pl
from jax.experimental.pallas import tpu as pltpu

F32 = jnp.float32
BF16 = jnp.bfloat16

HEAD_DIM = 128
PAGE_SIZE = 128
POOL_WINDOWS = (2, 4, 8, 16)
POOL_STATE = max(POOL_WINDOWS) - 1
POOL_HALO = 16
TOP_K = 4
SWIGLU_ALPHA = 1.702
SWIGLU_LIMIT = 7.0
NORM_EPS = 1e-5
LANES = 128
BF16_ROWS = 16
SLAB_ROWS = 8
VMEM_LIMIT = 56 * 1024 * 1024
NEG_INF = float("-inf")
EXPERT_TILE = 256
COMBINE_TILE = 128


DECODE_PAGES_PER_STEP = 8
FLASH_HEADS_PER_STEP = 2
XATTN_SEQS_PER_STEP = 8


def _divisor_tile(n, cap, multiple=8):
    for t in range(min(cap, n), 0, -1):
        if n % t == 0 and t % multiple == 0:
            return t
    raise ValueError(f"no tile for {n}")


def _cparams(*sem):
    return pltpu.CompilerParams(dimension_semantics=sem, vmem_limit_bytes=VMEM_LIMIT)


def _rms_f32(x, g):
    return x * lax.rsqrt(jnp.mean(x * x, axis=-1, keepdims=True) + NORM_EPS) * g


def _dot(a, b):
    return jnp.dot(a, b, preferred_element_type=F32)


def _dot_nt(a, b):
    return lax.dot_general(a, b, (((1,), (1,)), ((), ())), preferred_element_type=F32)


def _pad_rows(x):
    rows = x.shape[0]
    if rows >= BF16_ROWS:
        return x
    return jnp.concatenate([x, jnp.zeros((BF16_ROWS - rows,) + x.shape[1:], x.dtype)], axis=0)


def _split3(x):
    hi = x.astype(BF16)
    r = x - hi.astype(F32)
    mid = r.astype(BF16)
    lo = (r - mid.astype(F32)).astype(BF16)
    return hi, mid, lo


def _lane_cumsum(x, seg):
    n = x.shape[-1]
    r = lax.broadcasted_iota(jnp.int32, (n, n), 0)
    c = lax.broadcasted_iota(jnp.int32, (n, n), 1)
    keep = r <= c
    if seg < n:
        keep = jnp.logical_and(keep, (r // seg) == (c // seg))
    tri = jnp.where(keep, 1.0, 0.0).astype(BF16)
    rows = x.shape[0]
    hi, mid, lo = _split3(_pad_rows(x))
    return (_dot(hi, tri) + _dot(mid, tri) + _dot(lo, tri))[:rows]


def _log_sigmoid(z):
    return jnp.minimum(z, 0.0) - jnp.log1p(jnp.exp(-jnp.abs(z)))


def _in_proj_kernel(x_ref, g_ref, w_ref, wf_ref, bf_ref,
                    q_ref, k_ref, v_ref, u_ref, kb_ref, vb_ref, lf_ref, ct_ref,
                    xn_scr, carry_scr, *, tiles_per_seq, seg, heads):
    i = pl.program_id(0)
    j = pl.program_id(1)

    @pl.when(j == 0)
    def _():
        xn = _rms_f32(x_ref[...], g_ref[...]).astype(BF16)
        xn_scr[...] = xn
        lf = _log_sigmoid(_dot(xn, wf_ref[...]) + bf_ref[...])
        lf_ref[...] = lf[:, :heads]
        lft = lf.T[:8]
        c = _lane_cumsum(lft, seg)

        @pl.when(i % tiles_per_seq == 0)
        def _():
            carry_scr[...] = jnp.zeros_like(carry_scr)

        c = c + carry_scr[:, :1]
        ct_ref[0] = c
        carry_scr[...] = jnp.broadcast_to(c[:, -1:], carry_scr.shape)

    y = _dot(xn_scr[...], w_ref[...])

    @pl.when(j == 0)
    def _():
        q_ref[...] = y

    @pl.when(j == 1)
    def _():
        k_ref[...] = y
        kb_ref[...] = y.astype(BF16)

    @pl.when(j == 2)
    def _():
        v_ref[...] = y
        vb_ref[...] = y.astype(BF16)

    @pl.when(j == 3)
    def _():
        u_ref[...] = y


def _in_proj(x, g, w4, wf, bfp, *, seq_len, heads, tm):
    n, d = x.shape
    width = w4.shape[1] // 4
    if seq_len >= tm:
        tiles_per_seq, seg = seq_len // tm, tm
    else:
        tiles_per_seq, seg = 1, seq_len
    n_tiles = n // tm
    row = lambda i, j: (i, 0)
    big = jax.ShapeDtypeStruct((n, width), F32)
    bigb = jax.ShapeDtypeStruct((n, width), BF16)
    return pl.pallas_call(
        functools.partial(_in_proj_kernel, tiles_per_seq=tiles_per_seq, seg=seg, heads=heads),
        grid=(n_tiles, 4),
        in_specs=[
            pl.BlockSpec((tm, d), row),
            pl.BlockSpec((1, d), lambda i, j: (0, 0)),
            pl.BlockSpec((d, width), lambda i, j: (0, j)),
            pl.BlockSpec((d, LANES), lambda i, j: (0, 0)),
            pl.BlockSpec((1, LANES), lambda i, j: (0, 0)),
        ],
        out_specs=[
            pl.BlockSpec((tm, width), row), pl.BlockSpec((tm, width), row),
            pl.BlockSpec((tm, width), row), pl.BlockSpec((tm, width), row),
            pl.BlockSpec((tm, width), row), pl.BlockSpec((tm, width), row),
            pl.BlockSpec((tm, heads), row),
            pl.BlockSpec((1, 8, tm), lambda i, j: (i, 0, 0)),
        ],
        out_shape=[big, big, big, big, bigb, bigb,
                   jax.ShapeDtypeStruct((n, heads), F32),
                   jax.ShapeDtypeStruct((n_tiles, 8, tm), F32)],
        scratch_shapes=[pltpu.VMEM((tm, d), BF16), pltpu.VMEM((8, LANES), F32)],
        compiler_params=_cparams("arbitrary", "arbitrary"),
        name="in_proj",
    )(x, g, w4, wf, bfp)


def _fox_flash_kernel(qi_ref, ki_ref, q_ref, k_ref, v_ref, ct_ref, o_ref, m_scr, l_scr, acc_scr,
                      *, scale, heads_per_step):
    hp = pl.program_id(1)
    t = pl.program_id(2)
    qi = qi_ref[t]
    ki = ki_ref[t]

    @pl.when(ki == 0)
    def _():
        m_scr[...] = jnp.full_like(m_scr, NEG_INF)
        l_scr[...] = jnp.zeros_like(l_scr)
        acc_scr[...] = jnp.zeros_like(acc_scr)

    def step(diagonal):
        for hh in range(heads_per_step):
            sl = slice(hh * HEAD_DIM, (hh + 1) * HEAD_DIM)
            q = (q_ref[:, sl] * scale).astype(BF16)
            s = _dot_nt(q, k_ref[:, sl]) - ct_ref[0, pl.ds(hp * heads_per_step + hh, 1), :]
            if diagonal:
                r = lax.broadcasted_iota(jnp.int32, s.shape, 0)
                c = lax.broadcasted_iota(jnp.int32, s.shape, 1)
                s = jnp.where(c <= r, s, NEG_INF)
            m_prev = m_scr[hh]
            m_new = jnp.maximum(m_prev, jnp.max(s, axis=-1, keepdims=True))
            alpha = jnp.exp(m_prev - m_new)
            p = jnp.exp(s - m_new[:, :1])
            l_new = alpha * l_scr[hh] + jnp.sum(p, axis=-1, keepdims=True)
            acc = alpha * acc_scr[hh] + _dot(p.astype(BF16), v_ref[:, sl])
            if diagonal:
                o_ref[:, sl] = (acc / l_new).astype(o_ref.dtype)
            else:
                l_scr[hh] = l_new
                acc_scr[hh] = acc
                m_scr[hh] = m_new

    @pl.when(ki < qi)
    def _():
        step(False)

    @pl.when(ki == qi)
    def _():
        step(True)


def _fox_prompt(q, kb, vb, ct, *, batch, seq_len, heads, tq, heads_per_step):
    n, width = q.shape
    nq = seq_len // tq
    hw = heads_per_step * HEAD_DIM
    pairs = [(qi, ki) for qi in range(nq) for ki in range(qi + 1)]
    qi_tab = jnp.array([p[0] for p in pairs], jnp.int32)
    ki_tab = jnp.array([p[1] for p in pairs], jnp.int32)
    q_blk = lambda b, hp, t, qt, kt: (b * nq + qt[t], hp)
    kv_blk = lambda b, hp, t, qt, kt: (b * nq + kt[t], hp)
    grid_spec = pltpu.PrefetchScalarGridSpec(
        num_scalar_prefetch=2,
        grid=(batch, heads // heads_per_step, len(pairs)),
        in_specs=[
            pl.BlockSpec((tq, hw), q_blk),
            pl.BlockSpec((tq, hw), kv_blk),
            pl.BlockSpec((tq, hw), kv_blk),
            pl.BlockSpec((1, 8, tq), lambda b, hp, t, qt, kt: (b * nq + kt[t], 0, 0)),
        ],
        out_specs=pl.BlockSpec((tq, hw), q_blk),
        scratch_shapes=[pltpu.VMEM((heads_per_step, tq, LANES), F32),
                        pltpu.VMEM((heads_per_step, tq, LANES), F32),
                        pltpu.VMEM((heads_per_step, tq, HEAD_DIM), F32)],
    )
    return pl.pallas_call(
        functools.partial(_fox_flash_kernel, scale=HEAD_DIM ** -0.5, heads_per_step=heads_per_step),
        grid_spec=grid_spec,
        out_shape=jax.ShapeDtypeStruct((n, width), BF16),
        compiler_params=_cparams("arbitrary", "arbitrary", "arbitrary"),
        name="fox_prompt",
    )(qi_tab, ki_tab, q, kb, vb, ct)


def _logf_scan_kernel(x_ref, cum_ref, tot_ref, *, heads):
    n = x_ref.shape[1]
    r = lax.broadcasted_iota(jnp.int32, (n, n), 0)
    c = lax.broadcasted_iota(jnp.int32, (n, n), 1)
    same_head = jnp.bitwise_and(r, heads - 1) == jnp.bitwise_and(c, heads - 1)
    m_tot = jnp.where(same_head, 1.0, 0.0).astype(BF16)
    m_cum = jnp.where(jnp.logical_and(same_head, r <= c), 1.0, 0.0).astype(BF16)
    hi, mid, lo = _split3(x_ref[...])
    cum_ref[...] = _dot(hi, m_cum) + _dot(mid, m_cum) + _dot(lo, m_cum)
    tot_ref[...] = _dot(hi, m_tot) + _dot(mid, m_tot) + _dot(lo, m_tot)


def _logf_scan(logf_pages, *, heads, tm):
    n_pool, n = logf_pages.shape
    out = jax.ShapeDtypeStruct((n_pool, n), F32)
    return pl.pallas_call(
        functools.partial(_logf_scan_kernel, heads=heads),
        grid=(n_pool // tm,),
        in_specs=[pl.BlockSpec((tm, n), lambda i: (i, 0))],
        out_specs=[pl.BlockSpec((tm, n), lambda i: (i, 0))] * 2,
        out_shape=[out, out],
        compiler_params=_cparams("arbitrary"),
        name="logf_scan",
    )(logf_pages)


def _fox_decode_kernel(pt_ref, q_ref, *refs, scale, heads, n_new, pages_per_step):
    del pt_ref
    pps = pages_per_step
    k_refs, v_refs = refs[:pps], refs[pps:2 * pps]
    cum_refs, tot_refs = refs[2 * pps:3 * pps], refs[3 * pps:4 * pps]
    kn_ref, vn_ref, cn_ref, o_ref, m_scr, l_scr, acc_scr, carry_scr = refs[4 * pps:]
    step = pl.program_id(1)
    last = pl.num_programs(1) - 1
    rows = heads * n_new
    page_rows = PAGE_SIZE * heads

    @pl.when(step == 0)
    def _():
        m_scr[...] = jnp.full_like(m_scr, NEG_INF)
        l_scr[...] = jnp.zeros_like(l_scr)
        acc_scr[...] = jnp.zeros_like(acc_scr)
        carry_scr[...] = jnp.zeros_like(carry_scr)

    qb = jnp.concatenate([q_ref[:, h * HEAD_DIM:(h + 1) * HEAD_DIM] for h in range(heads)], axis=0)
    qb = (qb * scale).astype(BF16)

    def update(s, pv):
        m_prev = m_scr[...]
        m_new = jnp.maximum(m_prev, jnp.max(s, axis=-1, keepdims=True))
        alpha = jnp.exp(m_prev - m_new)
        p = jnp.exp(s - m_new[:, :1])
        l_scr[...] = alpha * l_scr[...] + jnp.sum(p, axis=-1, keepdims=True)
        acc_scr[...] = alpha * acc_scr[...] + pv(p.astype(BF16))
        m_scr[...] = m_new

    r = lax.broadcasted_iota(jnp.int32, (rows, page_rows), 0)
    c = lax.broadcasted_iota(jnp.int32, (rows, page_rows), 1)
    head_ok = jnp.bitwise_and(c, heads - 1) == r // n_new
    carry = carry_scr[...]
    parts = []
    for i in range(pps):
        s = _dot_nt(qb, k_refs[i][0].astype(BF16))
        parts.append(jnp.where(head_ok, s - (cum_refs[i][0] + carry), NEG_INF))
        carry = carry + tot_refs[i][0]
    carry_scr[...] = carry

    def page_values(p):
        out = _dot(p[:, :page_rows], v_refs[0][0].astype(BF16))
        for i in range(1, pps):
            out = out + _dot(p[:, i * page_rows:(i + 1) * page_rows], v_refs[i][0].astype(BF16))
        return out

    update(jnp.concatenate(parts, axis=1), page_values)

    @pl.when(step == last)
    def _():
        pad = jnp.zeros((LANES - rows, HEAD_DIM), F32)
        rn = lax.broadcasted_iota(jnp.int32, (rows, LANES), 0)
        cn = lax.broadcasted_iota(jnp.int32, (rows, LANES), 1)
        ok = jnp.logical_and(jnp.bitwise_and(cn, heads - 1) == rn // n_new,
                             cn // heads <= jnp.bitwise_and(rn, n_new - 1))
        s = _dot_nt(qb, jnp.concatenate([kn_ref[0], pad], axis=0).astype(BF16))
        s = jnp.where(ok, s - (cn_ref[0] + carry[:, :LANES]), NEG_INF)
        vn = jnp.concatenate([vn_ref[0], pad], axis=0).astype(BF16)
        update(s, lambda p: _dot(p, vn))
        out = acc_scr[...] / l_scr[...]
        for h in range(heads):
            o_ref[:, h * HEAD_DIM:(h + 1) * HEAD_DIM] = out[h * n_new:(h + 1) * n_new]


def _fox_decode(q, k_new, v_new, c_new, kc, vc, cum, tot, page_table, *, heads, n_new, pages_per_step):
    n, width = q.shape
    n_seq, n_pages = page_table.shape
    pps = pages_per_step
    page_rows = PAGE_SIZE * heads
    seq = lambda b, p, pt: (b, 0, 0)
    page = lambda i: (lambda b, p, pt: (pt[b * n_pages + p * pps + i], 0, 0))
    kv_specs = [pl.BlockSpec((1, page_rows, HEAD_DIM), page(i)) for i in range(pps)]
    row_specs = [pl.BlockSpec((1, 1, page_rows), page(i)) for i in range(pps)]
    grid_spec = pltpu.PrefetchScalarGridSpec(
        num_scalar_prefetch=1,
        grid=(n_seq, n_pages // pps),
        in_specs=[pl.BlockSpec((n_new, width), lambda b, p, pt: (b, 0))]
        + kv_specs + kv_specs + row_specs + row_specs
        + [pl.BlockSpec((1, n_new * heads, HEAD_DIM), seq), pl.BlockSpec((1, n_new * heads, HEAD_DIM), seq),
           pl.BlockSpec((1, 1, LANES), seq)],
        out_specs=pl.BlockSpec((n_new, width), lambda b, p, pt: (b, 0)),
        scratch_shapes=[pltpu.VMEM((heads * n_new, LANES), F32), pltpu.VMEM((heads * n_new, LANES), F32),
                        pltpu.VMEM((heads * n_new, HEAD_DIM), F32), pltpu.VMEM((1, page_rows), F32)],
    )
    return pl.pallas_call(
        functools.partial(_fox_decode_kernel, scale=HEAD_DIM ** -0.5, heads=heads, n_new=n_new,
                          pages_per_step=pps),
        grid_spec=grid_spec,
        out_shape=jax.ShapeDtypeStruct((n, width), F32),
        compiler_params=_cparams("arbitrary", "arbitrary"),
        name="fox_decode",
    )(page_table.reshape(-1), q, *([kc] * pps), *([vc] * pps), *([cum] * pps), *([tot] * pps),
      k_new, v_new, c_new)


def _pool_prompt_kernel(u_ref, halo_ref, w_ref, s_ref, o_ref, *, tiles_per_seq, gd):
    tm = u_ref.shape[0]
    tile_in_seq = pl.program_id(0) % tiles_per_seq
    pos = tile_in_seq * tm + lax.broadcasted_iota(jnp.int32, (tm, 1), 0)
    for g, w in enumerate(POOL_WINDOWS):
        sl = slice(g * gd, (g + 1) * gd)
        u = u_ref[:, sl]
        halo = jnp.where(tile_in_seq > 0, halo_ref[:, sl], 0.0)
        acc = jnp.concatenate([halo, u], axis=0)
        k = 1
        while k < w:
            acc = acc + pltpu.roll(acc, k, axis=0)
            k *= 2
        cnt = jnp.minimum(pos + 1, w).astype(F32)
        d = acc[POOL_HALO:] / cnt - u
        o_ref[:, sl] = (_dot(d.astype(BF16), w_ref[g]) * s_ref[:, sl]).astype(o_ref.dtype)


def _pool_prompt(u, w_pool, s_pool, *, seq_len, tm):
    n, pw = u.shape
    gd = pw // len(POOL_WINDOWS)
    halo_blocks = tm // POOL_HALO
    return pl.pallas_call(
        functools.partial(_pool_prompt_kernel, tiles_per_seq=seq_len // tm, gd=gd),
        grid=(n // tm,),
        in_specs=[
            pl.BlockSpec((tm, pw), lambda i: (i, 0)),
            pl.BlockSpec((POOL_HALO, pw), lambda i: (jnp.maximum(i * halo_blocks - 1, 0), 0)),
            pl.BlockSpec(w_pool.shape, lambda i: (0, 0, 0)),
            pl.BlockSpec((1, pw), lambda i: (0, 0)),
        ],
        out_specs=pl.BlockSpec((tm, pw), lambda i: (i, 0)),
        out_shape=jax.ShapeDtypeStruct((n, pw), BF16),
        compiler_params=_cparams("arbitrary"),
        name="pool_prompt",
    )(u, u, w_pool, s_pool)


def _pool_sample_kernel(u_ref, st_ref, w_ref, s_ref, o_ref, *, gd, past_len):
    n_new = u_ref.shape[0]

    def ext(j, sl):
        return st_ref[j, :, sl] if j < POOL_STATE else u_ref[j - POOL_STATE, :, sl]

    for g, w in enumerate(POOL_WINDOWS):
        sl = slice(g * gd, (g + 1) * gd)
        for t in range(n_new):
            cur = ext(POOL_STATE + t, sl)
            acc = cur
            for j in range(1, w):
                acc = acc + ext(POOL_STATE + t - j, sl)
            d = acc / float(min(past_len + t + 1, w)) - cur
            o_ref[t, :, sl] = (_dot(d.astype(BF16), w_ref[g]) * s_ref[:, sl]).astype(o_ref.dtype)


def _pool_sample(u_t, st_t, w_pool, s_pool, *, past_len):
    n_new, n_seq, pw = u_t.shape
    gd = pw // len(POOL_WINDOWS)
    return pl.pallas_call(
        functools.partial(_pool_sample_kernel, gd=gd, past_len=past_len),
        out_shape=jax.ShapeDtypeStruct((n_new, n_seq, pw), BF16),
        compiler_params=pltpu.CompilerParams(vmem_limit_bytes=VMEM_LIMIT),
        name="pool_sample",
    )(u_t, st_t, w_pool, s_pool)


def _mix_out_kernel(x_ref, a_ref, p_ref, wa_ref, wp_ref, g_ref, wq_ref, x1_ref, qx_ref):
    x1 = x_ref[...] + _dot(a_ref[...].astype(BF16), wa_ref[...]) + _dot(p_ref[...], wp_ref[...])
    x1_ref[...] = x1
    qx_ref[...] = _dot(_rms_f32(x1, g_ref[...]).astype(BF16), wq_ref[...])


def _mix_out(x, attn, pool, wa, wp, g, wq, *, tm):
    n, d = x.shape
    xw = wq.shape[1]
    row = lambda i: (i, 0)
    fixed = lambda i: (0, 0)
    return pl.pallas_call(
        _mix_out_kernel,
        grid=(n // tm,),
        in_specs=[
            pl.BlockSpec((tm, d), row), pl.BlockSpec((tm, attn.shape[1]), row),
            pl.BlockSpec((tm, pool.shape[1]), row),
            pl.BlockSpec(wa.shape, fixed), pl.BlockSpec(wp.shape, fixed),
            pl.BlockSpec((1, d), fixed), pl.BlockSpec(wq.shape, fixed),
        ],
        out_specs=[pl.BlockSpec((tm, d), row), pl.BlockSpec((tm, xw), row)],
        out_shape=[jax.ShapeDtypeStruct((n, d), F32), jax.ShapeDtypeStruct((n, xw), F32)],
        compiler_params=_cparams("arbitrary"),
        name="mix_out",
    )(x, attn, pool, wa, wp, g, wq)


def _mem_kv_kernel(m_ref, g_ref, w_ref, k_ref, v_ref):
    y = _dot(_rms_f32(m_ref[...], g_ref[...]).astype(BF16), w_ref[...])
    xw = k_ref.shape[1]
    k_ref[...] = y[:, :xw]
    v_ref[...] = y[:, xw:]


def _mem_kv(mem, g, wkv, *, tm):
    n, d = mem.shape
    xw = wkv.shape[1] // 2
    out = jax.ShapeDtypeStruct((n, xw), F32)
    return pl.pallas_call(
        _mem_kv_kernel,
        grid=(n // tm,),
        in_specs=[pl.BlockSpec((tm, d), lambda i: (i, 0)), pl.BlockSpec((1, d), lambda i: (0, 0)),
                  pl.BlockSpec(wkv.shape, lambda i: (0, 0))],
        out_specs=[pl.BlockSpec((tm, xw), lambda i: (i, 0))] * 2,
        out_shape=[out, out],
        compiler_params=_cparams("arbitrary"),
        name="mem_kv",
    )(mem, g, wkv)


def _xattn_kernel(q_ref, k_ref, v_ref, o_ref, *, scale, heads, groups):
    tq = q_ref.shape[0] // groups
    n_mem = k_ref.shape[0] // groups
    for g in range(groups):
        rows = slice(g * tq, (g + 1) * tq)
        mem = slice(g * n_mem, (g + 1) * n_mem)
        for h in range(heads):
            sl = slice(h * HEAD_DIM, (h + 1) * HEAD_DIM)
            q = _pad_rows(q_ref[rows, sl] * scale).astype(BF16)
            s = _dot_nt(q, k_ref[mem, sl].astype(BF16))
            p = jnp.exp(s - jnp.max(s, axis=-1, keepdims=True))
            o = _dot(p.astype(BF16), v_ref[mem, sl].astype(BF16)) / jnp.sum(p, axis=-1, keepdims=True)
            o_ref[rows, sl] = o[:tq]


def _xattn(q, mk, mv, *, rows_per_mem, n_mem, tm):
    n, xw = q.shape
    if tm <= rows_per_mem:
        groups, tiles_per_mem = 1, rows_per_mem // tm
        mem = lambda i: (i // tiles_per_mem, 0)
    else:
        groups = tm // rows_per_mem
        mem = lambda i: (i, 0)
    return pl.pallas_call(
        functools.partial(_xattn_kernel, scale=HEAD_DIM ** -0.5, heads=xw // HEAD_DIM, groups=groups),
        grid=(n // tm,),
        in_specs=[pl.BlockSpec((tm, xw), lambda i: (i, 0)),
                  pl.BlockSpec((groups * n_mem, xw), mem), pl.BlockSpec((groups * n_mem, xw), mem)],
        out_specs=pl.BlockSpec((tm, xw), lambda i: (i, 0)),
        out_shape=jax.ShapeDtypeStruct((n, xw), F32),
        compiler_params=_cparams("arbitrary"),
        name="xattn",
    )(q, mk, mv)


def _pack_slab(x):
    half = x.shape[1] // 2
    lo = lax.bitcast_convert_type(x[:, :half].astype(BF16).astype(F32), jnp.uint32)
    hi = lax.bitcast_convert_type(x[:, half:].astype(BF16).astype(F32), jnp.uint32)
    return jnp.bitwise_or(jnp.bitwise_and(hi, jnp.uint32(0xFFFF0000)), lax.shift_right_logical(lo, jnp.uint32(16)))


def _unpack_slab(u):
    lo = lax.bitcast_convert_type(lax.shift_left(u, jnp.uint32(16)), F32)
    hi = lax.bitcast_convert_type(jnp.bitwise_and(u, jnp.uint32(0xFFFF0000)), F32)
    return jnp.concatenate([lo, hi], axis=1)


def _store_slabs(ref, words):
    m = words.shape[0]
    for s in range(SLAB_ROWS):
        ref[pl.ds(s, m, stride=SLAB_ROWS), :] = words[:, s * LANES:(s + 1) * LANES]


def _load_slabs(ref, m, first_row=0):
    return jnp.concatenate(
        [ref[pl.ds(first_row + s, m, stride=SLAB_ROWS), :] for s in range(SLAB_ROWS)], axis=1)


def _xattn_out_kernel(x_ref, o_ref, wo_ref, g_ref, wr_ref, br_ref, x2_ref, xn_ref, lg_ref):
    x2 = x_ref[...] + _dot(o_ref[...].astype(BF16), wo_ref[...])
    x2_ref[...] = x2
    xn = _rms_f32(x2, g_ref[...])
    _store_slabs(xn_ref, _pack_slab(xn))
    lg_ref[...] = _dot(xn.astype(BF16), wr_ref[...]) + br_ref[...]


def _xattn_out(x, o, wo, g, wr, br, *, tm):
    n, d = x.shape
    assert d == 2 * SLAB_ROWS * LANES
    row = lambda i: (i, 0)
    fixed = lambda i: (0, 0)
    return pl.pallas_call(
        _xattn_out_kernel,
        grid=(n // tm,),
        in_specs=[pl.BlockSpec((tm, d), row), pl.BlockSpec((tm, o.shape[1]), row),
                  pl.BlockSpec(wo.shape, fixed), pl.BlockSpec((1, d), fixed),
                  pl.BlockSpec(wr.shape, fixed), pl.BlockSpec((1, LANES), fixed)],
        out_specs=[pl.BlockSpec((tm, d), row), pl.BlockSpec((tm * SLAB_ROWS, LANES), row),
                   pl.BlockSpec((tm, LANES), row)],
        out_shape=[jax.ShapeDtypeStruct((n, d), F32), jax.ShapeDtypeStruct((n * SLAB_ROWS, LANES), jnp.uint32),
                   jax.ShapeDtypeStruct((n, LANES), F32)],
        compiler_params=_cparams("arbitrary"),
        name="xattn_out",
    )(x, o, wo, g, wr, br)


def _slab(ref, row):
    return ref.at[pl.ds(pl.multiple_of(row * SLAB_ROWS, SLAB_ROWS), SLAB_ROWS)]


def _dispatch_kernel(dest_ref, x_ref, xs_in, xs_hbm, sem):
    del xs_in
    n_copies = dest_ref.shape[1]

    def slab_copy(i):
        return pltpu.make_async_copy(_slab(x_ref, i // TOP_K), _slab(xs_hbm, dest_ref[0, i]), sem)

    lax.fori_loop(0, n_copies, lambda i, c: (slab_copy(i).start(), c)[1], 0, unroll=8)
    lax.fori_loop(0, n_copies, lambda i, c: (slab_copy(i).wait(), c)[1], 0, unroll=8)


def _dispatch(xn_slabs, dest, xs, *, tc):
    n = xn_slabs.shape[0] // SLAB_ROWS
    return pl.pallas_call(
        _dispatch_kernel,
        grid=(n // tc,),
        in_specs=[pl.BlockSpec((None, 1, tc * TOP_K), lambda i: (i, 0, 0), memory_space=pltpu.SMEM),
                  pl.BlockSpec((tc * SLAB_ROWS, LANES), lambda i: (i, 0)),
                  pl.BlockSpec(memory_space=pl.ANY)],
        out_specs=pl.BlockSpec(memory_space=pl.ANY),
        out_shape=jax.ShapeDtypeStruct(xs.shape, xs.dtype),
        scratch_shapes=[pltpu.SemaphoreType.DMA(())],
        input_output_aliases={2: 0},
        compiler_params=_cparams("arbitrary"),
        name="moe_dispatch",
    )(dest, xn_slabs, xs)


def _gate_up_kernel(be_ref, nblk_ref, x_ref, wg_ref, wl_ref, bg_ref, bl_ref, h_ref):
    del be_ref

    used = pl.program_id(1) < nblk_ref[0]

    @pl.when(jnp.logical_not(used))
    def _():
        h_ref[...] = jnp.zeros_like(h_ref)

    @pl.when(used)
    def _():
        x = _unpack_slab(_load_slabs(x_ref, h_ref.shape[0])).astype(BF16)
        gate = jnp.minimum(_dot(x, wg_ref[...].astype(BF16)) + bg_ref[...], SWIGLU_LIMIT)
        lin = jnp.clip(_dot(x, wl_ref[...].astype(BF16)) + bl_ref[...], -SWIGLU_LIMIT, SWIGLU_LIMIT)
        h_ref[...] = ((lin + 1.0) * gate * jax.nn.sigmoid(SWIGLU_ALPHA * gate)).astype(h_ref.dtype)


def _gate_up(xs, w_gu, b_gu, blk_e, nblk, *, tm, tn):
    rows = xs.shape[0] // SLAB_ROWS
    n_exp, d, de2 = w_gu.shape
    de = de2 // 2
    nj = de // tn
    used = lambda j, r, be, nb: (jnp.minimum(r, nb[0] - 1), 0)
    exp_col = lambda off: (lambda j, r, be, nb: (be[jnp.minimum(r, nb[0] - 1)], 0, j + off))
    grid_spec = pltpu.PrefetchScalarGridSpec(
        num_scalar_prefetch=2,
        grid=(nj, rows // tm),
        in_specs=[pl.BlockSpec((tm * SLAB_ROWS, LANES), used),
                  pl.BlockSpec((None, d, tn), exp_col(0)), pl.BlockSpec((None, d, tn), exp_col(nj)),
                  pl.BlockSpec((None, 1, tn), exp_col(0)), pl.BlockSpec((None, 1, tn), exp_col(nj))],
        out_specs=pl.BlockSpec((tm, tn), lambda j, r, be, nb: (r, j)),
    )
    b3 = b_gu.reshape(n_exp, 1, de2)
    return pl.pallas_call(
        _gate_up_kernel,
        grid_spec=grid_spec,
        out_shape=jax.ShapeDtypeStruct((rows, de), BF16),
        compiler_params=_cparams("arbitrary", "arbitrary"),
        name="moe_gate_up",
    )(blk_e, nblk, xs, w_gu, w_gu, b3, b3)


def _down_kernel(be_ref, nblk_ref, h_ref, w_ref, b_ref, o_ref):
    del be_ref
    used = pl.program_id(0) < nblk_ref[0]

    @pl.when(jnp.logical_not(used))
    def _():
        o_ref[...] = jnp.zeros_like(o_ref)

    @pl.when(used)
    def _():
        out = _dot(h_ref[...], w_ref[...].astype(BF16)) + b_ref[...]
        _store_slabs(o_ref, _pack_slab(out))


def _down(hdn, w_down, b_down, blk_e, nblk, *, tm):
    rows, de = hdn.shape
    n_exp, _, d = w_down.shape
    assert d == 2 * SLAB_ROWS * LANES
    used = lambda r, be, nb: (jnp.minimum(r, nb[0] - 1), 0)
    expert = lambda r, be, nb: (be[jnp.minimum(r, nb[0] - 1)], 0, 0)
    grid_spec = pltpu.PrefetchScalarGridSpec(
        num_scalar_prefetch=2,
        grid=(rows // tm,),
        in_specs=[pl.BlockSpec((tm, de), used),
                  pl.BlockSpec((None, de, d), expert), pl.BlockSpec((None, 1, d), expert)],
        out_specs=pl.BlockSpec((tm * SLAB_ROWS, LANES), lambda r, be, nb: (r, 0)),
    )
    return pl.pallas_call(
        _down_kernel,
        grid_spec=grid_spec,
        out_shape=jax.ShapeDtypeStruct((rows * SLAB_ROWS, LANES), jnp.uint32),
        compiler_params=_cparams("arbitrary"),
        name="moe_down",
    )(blk_e, nblk, hdn, w_down, b_down.reshape(n_exp, 1, d))


def _combine_kernel(pos_ref, x_ref, gate_ref, rows_hbm, g_ref, y_ref, buf, sem):
    tc = x_ref.shape[0]

    def slab_copy(i):
        return pltpu.make_async_copy(_slab(rows_hbm, pos_ref[0, i]), _slab(buf, i), sem)

    lax.fori_loop(0, TOP_K * tc, lambda i, c: (slab_copy(i).start(), c)[1], 0, unroll=8)
    lax.fori_loop(0, TOP_K * tc, lambda i, c: (slab_copy(i).wait(), c)[1], 0, unroll=8)
    x3 = x_ref[...]
    for k in range(TOP_K):
        x3 = x3 + gate_ref[:, k:k + 1] * _unpack_slab(_load_slabs(buf, tc, first_row=k * tc * SLAB_ROWS))
    y_ref[...] = _rms_f32(x3, g_ref[...])


def _combine(x, gates, rows, pos, g, *, tc):
    n, d = x.shape
    return pl.pallas_call(
        _combine_kernel,
        grid=(n // tc,),
        in_specs=[pl.BlockSpec((None, 1, TOP_K * tc), lambda i: (i, 0, 0), memory_space=pltpu.SMEM),
                  pl.BlockSpec((tc, d), lambda i: (i, 0)),
                  pl.BlockSpec((tc, TOP_K), lambda i: (i, 0)),
                  pl.BlockSpec(memory_space=pl.ANY),
                  pl.BlockSpec((1, d), lambda i: (0, 0))],
        out_specs=pl.BlockSpec((tc, d), lambda i: (i, 0)),
        out_shape=jax.ShapeDtypeStruct((n, d), F32),
        scratch_shapes=[pltpu.VMEM((TOP_K * tc * SLAB_ROWS, LANES), jnp.uint32), pltpu.SemaphoreType.DMA(())],
        compiler_params=_cparams("arbitrary"),
        name="moe_combine",
    )(pos, x, gates, rows, g)


def _route(logits, n_exp, tm):
    n_tok = logits.shape[0]
    n_asg = n_tok * TOP_K
    top_val, top_idx = lax.top_k(logits, TOP_K)
    gates = jax.nn.softmax(top_val, axis=-1)
    e_flat = top_idx.reshape(-1).astype(jnp.int32)
    onehot = (e_flat[:, None] == jnp.arange(n_exp, dtype=jnp.int32)[None, :]).astype(jnp.int32)
    csum = jnp.cumsum(onehot, axis=0)
    rank = jnp.sum(onehot * csum, axis=1) - 1
    counts = csum[-1]
    padded = (counts + tm - 1) // tm * tm
    pad_ends = jnp.cumsum(padded)
    dest = ((pad_ends - padded)[e_flat] + rank).astype(jnp.int32).reshape(n_tok, TOP_K)
    n_blocks = -(-n_asg // tm) + n_exp
    blk_e = jnp.minimum(jnp.searchsorted(pad_ends, jnp.arange(n_blocks, dtype=jnp.int32) * tm, side="right"),
                        n_exp - 1).astype(jnp.int32)
    nblk = (pad_ends[-1] // tm).astype(jnp.int32).reshape(1)
    return gates, dest, blk_e, nblk, n_blocks


def _moe_final(x2_groups, xn_groups, logit_groups, w_gu, b_gu, w_down, b_down, g_final):
    n_exp = w_gu.shape[0]
    tm = min(EXPERT_TILE, TOP_K * sum(x.shape[0] for x in x2_groups))
    logits = jnp.concatenate(logit_groups, axis=0)[:, :n_exp]
    gates, dest, blk_e, nblk, n_blocks = _route(logits, n_exp, tm)
    xs = jnp.zeros((n_blocks * tm * SLAB_ROWS, LANES), jnp.uint32)
    start = 0
    for xn in xn_groups:
        n = xn.shape[0] // SLAB_ROWS
        tc = min(COMBINE_TILE, n)
        xs = _dispatch(xn, dest[start:start + n].reshape(n // tc, 1, tc * TOP_K), xs, tc=tc)
        start += n
    hdn = _gate_up(xs, w_gu, b_gu, blk_e, nblk, tm=tm, tn=min(1024, w_gu.shape[2] // 2))
    rows = _down(hdn, w_down, b_down, blk_e, nblk, tm=tm)
    outs, start = [], 0
    for x2 in x2_groups:
        n = x2.shape[0]
        tc = min(COMBINE_TILE, n)
        pos = dest[start:start + n].reshape(n // tc, tc, TOP_K).transpose(0, 2, 1).reshape(n // tc, 1, TOP_K * tc)
        outs.append(_combine(x2, gates[start:start + n], rows, pos, g_final, tc=tc))
        start += n
    return outs


def _layer_common(x, attn, pool, wa, wp, g_x, wq, mk, mv, rows_per_mem, xattn_tile,
                  wo, g_f, wr, br):
    n = x.shape[0]
    x1, qx = _mix_out(x, attn, pool, wa, wp, g_x, wq, tm=min(256, n))
    o = _xattn(qx, mk, mv, rows_per_mem=rows_per_mem, n_mem=mk.shape[0] * rows_per_mem // n,
               tm=xattn_tile)
    return _xattn_out(x1, o, wo, g_f, wr, br, tm=min(256, n))


def kernel(x_prompt, x_sample, mem_prompt, cache_k, cache_v, cache_logf, cache_mem_k, cache_mem_v,
           state_pool, page_table, g_mix, w_in, b_forget, w_pool, s_pool, w_out, g_xattn, g_mem,
           w_xq, w_xk, w_xv, w_xo, g_ffn, w_router, b_router, w_gu, b_gu, w_down, b_down, g_final):
    depth = w_in.shape[0]
    bp, tp, d = x_prompt.shape
    bs, ts, _ = x_sample.shape
    n_pool, page, heads, hd = cache_k.shape[1:]
    fw = heads * hd
    pw = state_pool.shape[-1]
    n_mem = mem_prompt.shape[1]
    xw = w_xq.shape[-1]
    n_exp = w_router.shape[-1]
    past_len = page_table.shape[1] * page
    assert hd == HEAD_DIM and page == PAGE_SIZE and heads == 8 and pw == fw
    assert w_in.shape[-1] == 3 * fw + heads + pw and state_pool.shape[2] == POOL_STATE
    assert depth == 1, "the experts of all groups are evaluated together after the last layer"

    xp = x_prompt.reshape(bp * tp, d)
    xs = x_sample.reshape(bs * ts, d)
    row = lambda a: a.reshape(1, -1)
    outs = {k: [] for k in ("kp", "vp", "lfp", "pp", "mkp", "mvp", "ks", "vs", "lfs", "ps")}
    for l in range(depth):
        w = w_in[l]
        w4 = jnp.concatenate([w[:, :3 * fw], w[:, 3 * fw + heads:]], axis=1).astype(BF16)
        wf = jnp.pad(w[:, 3 * fw:3 * fw + heads], ((0, 0), (0, LANES - heads))).astype(BF16)
        bfp = jnp.pad(b_forget[l], (0, LANES - heads)).reshape(1, LANES)
        wpool = w_pool[l].astype(BF16)
        spool = row(s_pool[l])
        wa = w_out[l, :fw].astype(BF16)
        wp = w_out[l, fw:].astype(BF16)
        wq = w_xq[l].astype(BF16)
        wkv = jnp.concatenate([w_xk[l], w_xv[l]], axis=1).astype(BF16)
        wo = w_xo[l].astype(BF16)
        wr = jnp.pad(w_router[l], ((0, 0), (0, LANES - n_exp))).astype(BF16)
        br = jnp.pad(b_router[l], (0, LANES - n_exp)).reshape(1, LANES)

        tm = min(512, tp)
        q, k, v, u, kb, vb, lf, ct = _in_proj(xp, row(g_mix[l]), w4, wf, bfp, seq_len=tp, heads=heads, tm=tm)
        attn = _fox_prompt(q, kb, vb, ct, batch=bp, seq_len=tp, heads=heads, tq=tm,
                           heads_per_step=FLASH_HEADS_PER_STEP)
        pool = _pool_prompt(u, wpool, spool, seq_len=tp, tm=tm)
        outs["kp"].append(k.reshape(bp, tp, heads, hd))
        outs["vp"].append(v.reshape(bp, tp, heads, hd))
        outs["lfp"].append(lf.reshape(bp, tp, heads))
        outs["pp"].append(u.reshape(bp, tp, pw)[:, tp - POOL_STATE:])
        mk, mv = _mem_kv(mem_prompt.reshape(bp * n_mem, d), row(g_mem[l]), wkv, tm=min(256, bp * n_mem))
        outs["mkp"].append(mk.reshape(bp, n_mem, xw // HEAD_DIM, HEAD_DIM))
        outs["mvp"].append(mv.reshape(bp, n_mem, xw // HEAD_DIM, HEAD_DIM))
        x2p, xnp_, lgp = _layer_common(xp, attn, pool, wa, wp, row(g_xattn[l]), wq, mk, mv, tp, tm,
                                       wo, row(g_ffn[l]), wr, br)

        ns = bs * ts
        tms = min(512, ns)
        q, k, v, u, _, _, lf, ct = _in_proj(xs, row(g_mix[l]), w4, wf, bfp, seq_len=ts, heads=heads, tm=tms)
        c_new = ct.reshape(ns // tms, heads, tms // ts, ts).transpose(0, 2, 3, 1).reshape(bs, 1, ts * heads)
        c_new = jnp.pad(c_new, ((0, 0), (0, 0), (0, LANES - ts * heads)))
        cum, tot = _logf_scan(cache_logf[l].reshape(n_pool, page * heads), heads=heads,
                              tm=_divisor_tile(n_pool, 512))
        attn = _fox_decode(q, k.reshape(bs, ts * heads, hd), v.reshape(bs, ts * heads, hd), c_new,
                           cache_k[l].reshape(n_pool, page * heads, hd),
                           cache_v[l].reshape(n_pool, page * heads, hd),
                           cum.reshape(n_pool, 1, page * heads), tot.reshape(n_pool, 1, page * heads),
                           page_table, heads=heads, n_new=ts,
                           pages_per_step=_divisor_tile(page_table.shape[1], DECODE_PAGES_PER_STEP, 1))
        u3 = u.reshape(bs, ts, pw)
        pool = _pool_sample(u3.transpose(1, 0, 2), state_pool[l].transpose(1, 0, 2), wpool, spool,
                            past_len=past_len)
        pool = pool.transpose(1, 0, 2).reshape(ns, pw)
        outs["ks"].append(k.reshape(bs, ts, heads, hd))
        outs["vs"].append(v.reshape(bs, ts, heads, hd))
        outs["lfs"].append(lf.reshape(bs, ts, heads))
        outs["ps"].append(jnp.concatenate([state_pool[l], u3], axis=1)[:, ts:])
        mks = cache_mem_k[l].reshape(bs * n_mem, xw)
        mvs = cache_mem_v[l].reshape(bs * n_mem, xw)
        x2s, xns, lgs = _layer_common(xs, attn, pool, wa, wp, row(g_xattn[l]), wq, mks, mvs, ts,
                                      ts * _divisor_tile(bs, XATTN_SEQS_PER_STEP, 1),
                                      wo, row(g_ffn[l]), wr, br)

        yp, ys = _moe_final([x2p, x2s], [xnp_, xns], [lgp, lgs], w_gu[l], b_gu[l], w_down[l], b_down[l],
                            row(g_final))

    st = lambda name: jnp.stack(outs[name])
    return (yp.reshape(bp, tp, d), ys.reshape(bs, ts, d),
            st("kp"), st("vp"), st("lfp"), st("pp"), st("mkp"), st("mvp"),
            st("ks"), st("vs"), st("lfs"), st("ps"))
```

```python
import functools

import jax
import jax.numpy as jnp
from jax import lax
from jax.experimental import pallas as pl
from jax.experimental.pallas import tpu as pltpu

F32 = jnp.float32
BF16 = jnp.bfloat16

HEAD_DIM = 128
PAGE_SIZE = 128
POOL_WINDOWS = (2, 4, 8, 16)
POOL_STATE = max(POOL_WINDOWS) - 1
POOL_HALO = 16
TOP_K = 4
SWIGLU_ALPHA = 1.702
SWIGLU_LIMIT = 7.0
NORM_EPS = 1e-5
LANES = 128
BF16_ROWS = 16
SLAB_ROWS = 16
SCAN_ROWS = 8
VMEM_LIMIT = 56 * 1024 * 1024
NEG_INF = float("-inf")
EXPERT_TILE = 512
COMBINE_TILE = 128


DECODE_PAGES_PER_STEP = 8
FLASH_HEADS_PER_STEP = 4
XATTN_SEQS_PER_STEP = 8


def _divisor_tile(n, cap, multiple=8):
    for t in range(min(cap, n), 0, -1):
        if n % t == 0 and t % multiple == 0:
            return t
    raise ValueError(f"no tile for {n}")


def _cparams(*sem):
    return pltpu.CompilerParams(dimension_semantics=sem, vmem_limit_bytes=VMEM_LIMIT)


def _rms_f32(x, g):
    return x * lax.rsqrt(jnp.mean(x * x, axis=-1, keepdims=True) + NORM_EPS) * g


def _dot(a, b):
    return jnp.dot(a, b, preferred_element_type=F32)


def _dot_nt(a, b):
    return lax.dot_general(a, b, (((1,), (1,)), ((), ())), preferred_element_type=F32)


def _pad_rows(x):
    rows = x.shape[0]
    if rows >= BF16_ROWS:
        return x
    return jnp.concatenate([x, jnp.zeros((BF16_ROWS - rows,) + x.shape[1:], x.dtype)], axis=0)


def _split3(x):
    hi = x.astype(BF16)
    r = x - hi.astype(F32)
    mid = r.astype(BF16)
    lo = (r - mid.astype(F32)).astype(BF16)
    return hi, mid, lo


def _lane_cumsum(x, seg):
    n = x.shape[-1]
    r = lax.broadcasted_iota(jnp.int32, (n, n), 0)
    c = lax.broadcasted_iota(jnp.int32, (n, n), 1)
    keep = r <= c
    if seg < n:
        keep = jnp.logical_and(keep, (r // seg) == (c // seg))
    tri = jnp.where(keep, 1.0, 0.0).astype(BF16)
    rows = x.shape[0]
    hi, mid, lo = _split3(_pad_rows(x))
    return (_dot(hi, tri) + _dot(mid, tri) + _dot(lo, tri))[:rows]


def _log_sigmoid(z):
    return jnp.minimum(z, 0.0) - jnp.log1p(jnp.exp(-jnp.abs(z)))


def _in_proj_kernel(x_ref, g_ref, w_ref, wf_ref, bf_ref,
                    q_ref, k_ref, v_ref, u_ref, kb_ref, vb_ref, lf_ref, ct_ref,
                    xn_scr, carry_scr, *, tiles_per_seq, seg, heads):
    i = pl.program_id(0)
    j = pl.program_id(1)

    @pl.when(j == 0)
    def _():
        xn = _rms_f32(x_ref[...], g_ref[...]).astype(BF16)
        xn_scr[...] = xn
        lf = _log_sigmoid(_dot(xn, wf_ref[...]) + bf_ref[...])
        lf_ref[...] = lf[:, :heads]
        lft = lf.T[:8]
        c = _lane_cumsum(lft, seg)

        @pl.when(i % tiles_per_seq == 0)
        def _():
            carry_scr[...] = jnp.zeros_like(carry_scr)

        c = c + carry_scr[:, :1]
        ct_ref[0] = c
        carry_scr[...] = jnp.broadcast_to(c[:, -1:], carry_scr.shape)

    y = _dot(xn_scr[...], w_ref[...])

    @pl.when(j == 0)
    def _():
        q_ref[...] = y

    @pl.when(j == 1)
    def _():
        k_ref[...] = y
        kb_ref[...] = y.astype(BF16)

    @pl.when(j == 2)
    def _():
        v_ref[...] = y
        vb_ref[...] = y.astype(BF16)

    @pl.when(j == 3)
    def _():
        u_ref[...] = y


def _in_proj(x, g, w4, wf, bfp, *, seq_len, heads, tm):
    n, d = x.shape
    width = w4.shape[1] // 4
    if seq_len >= tm:
        tiles_per_seq, seg = seq_len // tm, tm
    else:
        tiles_per_seq, seg = 1, seq_len
    n_tiles = n // tm
    row = lambda i, j: (i, 0)
    big = jax.ShapeDtypeStruct((n, width), F32)
    bigb = jax.ShapeDtypeStruct((n, width), BF16)
    return pl.pallas_call(
        functools.partial(_in_proj_kernel, tiles_per_seq=tiles_per_seq, seg=seg, heads=heads),
        grid=(n_tiles, 4),
        in_specs=[
            pl.BlockSpec((tm, d), row),
            pl.BlockSpec((1, d), lambda i, j: (0, 0)),
            pl.BlockSpec((d, width), lambda i, j: (0, j)),
            pl.BlockSpec((d, LANES), lambda i, j: (0, 0)),
            pl.BlockSpec((1, LANES), lambda i, j: (0, 0)),
        ],
        out_specs=[
            pl.BlockSpec((tm, width), row), pl.BlockSpec((tm, width), row),
            pl.BlockSpec((tm, width), row), pl.BlockSpec((tm, width), row),
            pl.BlockSpec((tm, width), row), pl.BlockSpec((tm, width), row),
            pl.BlockSpec((tm, heads), row),
            pl.BlockSpec((1, 8, tm), lambda i, j: (i, 0, 0)),
        ],
        out_shape=[big, big, big, big, bigb, bigb,
                   jax.ShapeDtypeStruct((n, heads), F32),
                   jax.ShapeDtypeStruct((n_tiles, 8, tm), F32)],
        scratch_shapes=[pltpu.VMEM((tm, d), BF16), pltpu.VMEM((8, LANES), F32)],
        compiler_params=_cparams("arbitrary", "arbitrary"),
        name="in_proj",
    )(x, g, w4, wf, bfp)


def _fox_flash_kernel(qi_ref, ki_ref, q_ref, k_ref, v_ref, ct_ref, o_ref, m_scr, l_scr, acc_scr,
                      *, scale, heads_per_step):
    hp = pl.program_id(1)
    t = pl.program_id(2)
    qi = qi_ref[t]
    ki = ki_ref[t]

    @pl.when(ki == 0)
    def _():
        m_scr[...] = jnp.full_like(m_scr, NEG_INF)
        l_scr[...] = jnp.zeros_like(l_scr)
        acc_scr[...] = jnp.zeros_like(acc_scr)

    def step(diagonal):
        for hh in range(heads_per_step):
            sl = slice(hh * HEAD_DIM, (hh + 1) * HEAD_DIM)
            q = (q_ref[:, sl] * scale).astype(BF16)
            s = _dot_nt(q, k_ref[:, sl]) - ct_ref[0, pl.ds(hp * heads_per_step + hh, 1), :]
            if diagonal:
                r = lax.broadcasted_iota(jnp.int32, s.shape, 0)
                c = lax.broadcasted_iota(jnp.int32, s.shape, 1)
                s = jnp.where(c <= r, s, NEG_INF)
            m_prev = m_scr[hh]
            m_new = jnp.maximum(m_prev, jnp.max(s, axis=-1, keepdims=True))
            alpha = jnp.exp(m_prev - m_new)
            p = jnp.exp(s - m_new[:, :1])
            l_new = alpha * l_scr[hh] + jnp.sum(p, axis=-1, keepdims=True)
            acc = alpha * acc_scr[hh] + _dot(p.astype(BF16), v_ref[:, sl])
            if diagonal:
                o_ref[:, sl] = (acc / l_new).astype(o_ref.dtype)
            else:
                l_scr[hh] = l_new
                acc_scr[hh] = acc
                m_scr[hh] = m_new

    @pl.when(ki < qi)
    def _():
        step(False)

    @pl.when(ki == qi)
    def _():
        step(True)


def _fox_prompt(q, kb, vb, ct, *, batch, seq_len, heads, tq, heads_per_step):
    n, width = q.shape
    nq = seq_len // tq
    hw = heads_per_step * HEAD_DIM
    pairs = [(qi, ki) for qi in range(nq) for ki in range(qi + 1)]
    qi_tab = jnp.array([p[0] for p in pairs], jnp.int32)
    ki_tab = jnp.array([p[1] for p in pairs], jnp.int32)
    q_blk = lambda b, hp, t, qt, kt: (b * nq + qt[t], hp)
    kv_blk = lambda b, hp, t, qt, kt: (b * nq + kt[t], hp)
    grid_spec = pltpu.PrefetchScalarGridSpec(
        num_scalar_prefetch=2,
        grid=(batch, heads // heads_per_step, len(pairs)),
        in_specs=[
            pl.BlockSpec((tq, hw), q_blk),
            pl.BlockSpec((tq, hw), kv_blk),
            pl.BlockSpec((tq, hw), kv_blk),
            pl.BlockSpec((1, 8, tq), lambda b, hp, t, qt, kt: (b * nq + kt[t], 0, 0)),
        ],
        out_specs=pl.BlockSpec((tq, hw), q_blk),
        scratch_shapes=[pltpu.VMEM((heads_per_step, tq, LANES), F32),
                        pltpu.VMEM((heads_per_step, tq, LANES), F32),
                        pltpu.VMEM((heads_per_step, tq, HEAD_DIM), F32)],
    )
    return pl.pallas_call(
        functools.partial(_fox_flash_kernel, scale=HEAD_DIM ** -0.5, heads_per_step=heads_per_step),
        grid_spec=grid_spec,
        out_shape=jax.ShapeDtypeStruct((n, width), BF16),
        compiler_params=_cparams("arbitrary", "arbitrary", "arbitrary"),
        name="fox_prompt",
    )(qi_tab, ki_tab, q, kb, vb, ct)


def _logf_scan_kernel(x_ref, o_ref, *, heads):
    n = x_ref.shape[1]
    r = lax.broadcasted_iota(jnp.int32, (n, n), 0)
    c = lax.broadcasted_iota(jnp.int32, (n, n), 1)
    same_head = jnp.bitwise_and(r, heads - 1) == jnp.bitwise_and(c, heads - 1)
    m_tot = jnp.where(same_head, 1.0, 0.0).astype(BF16)
    m_cum = jnp.where(jnp.logical_and(same_head, r <= c), 1.0, 0.0).astype(BF16)
    hi, mid, lo = _split3(x_ref[...])
    o_ref[:, :n] = _dot(hi, m_cum) + _dot(mid, m_cum) + _dot(lo, m_cum)
    o_ref[:, n:] = _dot(hi, m_tot) + _dot(mid, m_tot) + _dot(lo, m_tot)


def _logf_scan(logf_pages, *, heads, tm):
    n_pool, n = logf_pages.shape
    return pl.pallas_call(
        functools.partial(_logf_scan_kernel, heads=heads),
        grid=(n_pool // tm,),
        in_specs=[pl.BlockSpec((tm, n), lambda i: (i, 0))],
        out_specs=pl.BlockSpec((tm, 2 * n), lambda i: (i, 0)),
        out_shape=jax.ShapeDtypeStruct((n_pool, 2 * n), F32),
        compiler_params=_cparams("arbitrary"),
        name="logf_scan",
    )(logf_pages)


def _fox_decode_kernel(pt_ref, q_ref, *refs, scale, heads, n_new, pages_per_step):
    pps = pages_per_step
    k_refs, v_refs, scan_refs = refs[:pps], refs[pps:2 * pps], refs[2 * pps:3 * pps]
    kn_ref, vn_ref, cn_ref, o_ref, m_scr, l_scr, acc_scr, carry_scr = refs[3 * pps:]
    step = pl.program_id(1)
    last = pl.num_programs(1) - 1
    rows = heads * n_new
    page_rows = PAGE_SIZE * heads

    @pl.when(step == 0)
    def _():
        m_scr[...] = jnp.full_like(m_scr, NEG_INF)
        l_scr[...] = jnp.zeros_like(l_scr)
        acc_scr[...] = jnp.zeros_like(acc_scr)
        carry_scr[...] = jnp.zeros_like(carry_scr)

    qb = jnp.concatenate([q_ref[:, h * HEAD_DIM:(h + 1) * HEAD_DIM] for h in range(heads)], axis=0)
    qb = (qb * scale).astype(BF16)

    def update(s, pv):
        m_prev = m_scr[...]
        m_new = jnp.maximum(m_prev, jnp.max(s, axis=-1, keepdims=True))
        alpha = jnp.exp(m_prev - m_new)
        p = jnp.exp(s - m_new[:, :1])
        l_scr[...] = alpha * l_scr[...] + jnp.sum(p, axis=-1, keepdims=True)
        acc_scr[...] = alpha * acc_scr[...] + pv(p.astype(BF16))
        m_scr[...] = m_new

    r = lax.broadcasted_iota(jnp.int32, (rows, page_rows), 0)
    c = lax.broadcasted_iota(jnp.int32, (rows, page_rows), 1)
    head_ok = jnp.bitwise_and(c, heads - 1) == r // n_new
    carry = carry_scr[...]
    parts = []
    for i in range(pps):
        s = _dot_nt(qb, k_refs[i][0].astype(BF16))
        page = pt_ref[(pl.program_id(0) * pl.num_programs(1) + step) * pps + i]
        scan = scan_refs[i][pl.ds(page % SCAN_ROWS, 1), :]
        parts.append(jnp.where(head_ok, s - (scan[:, :page_rows] + carry), NEG_INF))
        carry = carry + scan[:, page_rows:]
    carry_scr[...] = carry

    def page_values(p):
        out = _dot(p[:, :page_rows], v_refs[0][0].astype(BF16))
        for i in range(1, pps):
            out = out + _dot(p[:, i * page_rows:(i + 1) * page_rows], v_refs[i][0].astype(BF16))
        return out

    update(jnp.concatenate(parts, axis=1), page_values)

    @pl.when(step == last)
    def _():
        pad = jnp.zeros((LANES - rows, HEAD_DIM), F32)
        rn = lax.broadcasted_iota(jnp.int32, (rows, LANES), 0)
        cn = lax.broadcasted_iota(jnp.int32, (rows, LANES), 1)
        ok = jnp.logical_and(jnp.bitwise_and(cn, heads - 1) == rn // n_new,
                             cn // heads <= jnp.bitwise_and(rn, n_new - 1))
        s = _dot_nt(qb, jnp.concatenate([kn_ref[0], pad], axis=0).astype(BF16))
        s = jnp.where(ok, s - (cn_ref[0] + carry[:, :LANES]), NEG_INF)
        vn = jnp.concatenate([vn_ref[0], pad], axis=0).astype(BF16)
        update(s, lambda p: _dot(p, vn))
        out = acc_scr[...] / l_scr[...]
        for h in range(heads):
            o_ref[:, h * HEAD_DIM:(h + 1) * HEAD_DIM] = out[h * n_new:(h + 1) * n_new]


def _fox_decode(q, k_new, v_new, c_new, kc, vc, scan, page_table, *, heads, n_new, pages_per_step):
    n, width = q.shape
    n_seq, n_pages = page_table.shape
    pps = pages_per_step
    page_rows = PAGE_SIZE * heads
    seq = lambda b, p, pt: (b, 0, 0)
    page = lambda i: (lambda b, p, pt: (pt[b * n_pages + p * pps + i], 0, 0))
    kv_specs = [pl.BlockSpec((1, page_rows, HEAD_DIM), page(i)) for i in range(pps)]
    scan_page = lambda i: (lambda b, p, pt: (pt[b * n_pages + p * pps + i] // SCAN_ROWS, 0))
    scan_specs = [pl.BlockSpec((SCAN_ROWS, 2 * page_rows), scan_page(i)) for i in range(pps)]
    grid_spec = pltpu.PrefetchScalarGridSpec(
        num_scalar_prefetch=1,
        grid=(n_seq, n_pages // pps),
        in_specs=[pl.BlockSpec((n_new, width), lambda b, p, pt: (b, 0))]
        + kv_specs + kv_specs + scan_specs
        + [pl.BlockSpec((1, n_new * heads, HEAD_DIM), seq), pl.BlockSpec((1, n_new * heads, HEAD_DIM), seq),
           pl.BlockSpec((1, 1, LANES), seq)],
        out_specs=pl.BlockSpec((n_new, width), lambda b, p, pt: (b, 0)),
        scratch_shapes=[pltpu.VMEM((heads * n_new, LANES), F32), pltpu.VMEM((heads * n_new, LANES), F32),
                        pltpu.VMEM((heads * n_new, HEAD_DIM), F32), pltpu.VMEM((1, page_rows), F32)],
    )
    return pl.pallas_call(
        functools.partial(_fox_decode_kernel, scale=HEAD_DIM ** -0.5, heads=heads, n_new=n_new,
                          pages_per_step=pps),
        grid_spec=grid_spec,
        out_shape=jax.ShapeDtypeStruct((n, width), F32),
        compiler_params=_cparams("arbitrary", "arbitrary"),
        name="fox_decode",
    )(page_table.reshape(-1), q, *([kc] * pps), *([vc] * pps), *([scan] * pps), k_new, v_new, c_new)


def _pool_prompt_kernel(u_ref, halo_ref, w_ref, s_ref, o_ref, *, tiles_per_seq, gd):
    tm = u_ref.shape[0]
    tile_in_seq = pl.program_id(0) % tiles_per_seq
    pos = tile_in_seq * tm + lax.broadcasted_iota(jnp.int32, (tm, 1), 0)
    for g, w in enumerate(POOL_WINDOWS):
        sl = slice(g * gd, (g + 1) * gd)
        u = u_ref[:, sl]
        halo = jnp.where(tile_in_seq > 0, halo_ref[:, sl], 0.0)
        acc = jnp.concatenate([halo, u], axis=0)
        k = 1
        while k < w:
            acc = acc + pltpu.roll(acc, k, axis=0)
            k *= 2
        cnt = jnp.minimum(pos + 1, w).astype(F32)
        d = acc[POOL_HALO:] / cnt - u
        o_ref[:, sl] = (_dot(d.astype(BF16), w_ref[g]) * s_ref[:, sl]).astype(o_ref.dtype)


def _pool_prompt(u, w_pool, s_pool, *, seq_len, tm):
    n, pw = u.shape
    gd = pw // len(POOL_WINDOWS)
    halo_blocks = tm // POOL_HALO
    return pl.pallas_call(
        functools.partial(_pool_prompt_kernel, tiles_per_seq=seq_len // tm, gd=gd),
        grid=(n // tm,),
        in_specs=[
            pl.BlockSpec((tm, pw), lambda i: (i, 0)),
            pl.BlockSpec((POOL_HALO, pw), lambda i: (jnp.maximum(i * halo_blocks - 1, 0), 0)),
            pl.BlockSpec(w_pool.shape, lambda i: (0, 0, 0)),
            pl.BlockSpec((1, pw), lambda i: (0, 0)),
        ],
        out_specs=pl.BlockSpec((tm, pw), lambda i: (i, 0)),
        out_shape=jax.ShapeDtypeStruct((n, pw), BF16),
        compiler_params=_cparams("arbitrary"),
        name="pool_prompt",
    )(u, u, w_pool, s_pool)


def _pool_sample_kernel(u_ref, st_ref, w_ref, s_ref, o_ref, *, gd, past_len):
    n_new = u_ref.shape[0]

    def ext(j, sl):
        return st_ref[j, :, sl] if j < POOL_STATE else u_ref[j - POOL_STATE, :, sl]

    for g, w in enumerate(POOL_WINDOWS):
        sl = slice(g * gd, (g + 1) * gd)
        for t in range(n_new):
            cur = ext(POOL_STATE + t, sl)
            acc = cur
            for j in range(1, w):
                acc = acc + ext(POOL_STATE + t - j, sl)
            d = acc / float(min(past_len + t + 1, w)) - cur
            o_ref[t, :, sl] = (_dot(d.astype(BF16), w_ref[g]) * s_ref[:, sl]).astype(o_ref.dtype)


def _pool_sample(u_t, st_t, w_pool, s_pool, *, past_len):
    n_new, n_seq, pw = u_t.shape
    gd = pw // len(POOL_WINDOWS)
    return pl.pallas_call(
        functools.partial(_pool_sample_kernel, gd=gd, past_len=past_len),
        out_shape=jax.ShapeDtypeStruct((n_new, n_seq, pw), BF16),
        compiler_params=pltpu.CompilerParams(vmem_limit_bytes=VMEM_LIMIT),
        name="pool_sample",
    )(u_t, st_t, w_pool, s_pool)


def _mix_out_kernel(x_ref, a_ref, p_ref, wa_ref, wp_ref, g_ref, wq_ref, x1_ref, qx_ref):
    x1 = x_ref[...] + _dot(a_ref[...].astype(BF16), wa_ref[...]) + _dot(p_ref[...], wp_ref[...])
    x1_ref[...] = x1
    qx_ref[...] = _dot(_rms_f32(x1, g_ref[...]).astype(BF16), wq_ref[...])


def _mix_out(x, attn, pool, wa, wp, g, wq, *, tm):
    n, d = x.shape
    xw = wq.shape[1]
    row = lambda i: (i, 0)
    fixed = lambda i: (0, 0)
    return pl.pallas_call(
        _mix_out_kernel,
        grid=(n // tm,),
        in_specs=[
            pl.BlockSpec((tm, d), row), pl.BlockSpec((tm, attn.shape[1]), row),
            pl.BlockSpec((tm, pool.shape[1]), row),
            pl.BlockSpec(wa.shape, fixed), pl.BlockSpec(wp.shape, fixed),
            pl.BlockSpec((1, d), fixed), pl.BlockSpec(wq.shape, fixed),
        ],
        out_specs=[pl.BlockSpec((tm, d), row), pl.BlockSpec((tm, xw), row)],
        out_shape=[jax.ShapeDtypeStruct((n, d), F32), jax.ShapeDtypeStruct((n, xw), F32)],
        compiler_params=_cparams("arbitrary"),
        name="mix_out",
    )(x, attn, pool, wa, wp, g, wq)


def _mem_kv_kernel(m_ref, g_ref, w_ref, k_ref, v_ref):
    y = _dot(_rms_f32(m_ref[...], g_ref[...]).astype(BF16), w_ref[...])
    xw = k_ref.shape[1]
    k_ref[...] = y[:, :xw]
    v_ref[...] = y[:, xw:]


def _mem_kv(mem, g, wkv, *, tm):
    n, d = mem.shape
    xw = wkv.shape[1] // 2
    out = jax.ShapeDtypeStruct((n, xw), F32)
    return pl.pallas_call(
        _mem_kv_kernel,
        grid=(n // tm,),
        in_specs=[pl.BlockSpec((tm, d), lambda i: (i, 0)), pl.BlockSpec((1, d), lambda i: (0, 0)),
                  pl.BlockSpec(wkv.shape, lambda i: (0, 0))],
        out_specs=[pl.BlockSpec((tm, xw), lambda i: (i, 0))] * 2,
        out_shape=[out, out],
        compiler_params=_cparams("arbitrary"),
        name="mem_kv",
    )(mem, g, wkv)


def _xattn_kernel(q_ref, k_ref, v_ref, o_ref, *, scale, heads, groups):
    tq = q_ref.shape[0] // groups
    n_mem = k_ref.shape[0] // groups
    for g in range(groups):
        rows = slice(g * tq, (g + 1) * tq)
        mem = slice(g * n_mem, (g + 1) * n_mem)
        for h in range(heads):
            sl = slice(h * HEAD_DIM, (h + 1) * HEAD_DIM)
            q = _pad_rows(q_ref[rows, sl] * scale).astype(BF16)
            s = _dot_nt(q, k_ref[mem, sl].astype(BF16))
            p = jnp.exp(s - jnp.max(s, axis=-1, keepdims=True))
            o = _dot(p.astype(BF16), v_ref[mem, sl].astype(BF16)) / jnp.sum(p, axis=-1, keepdims=True)
            o_ref[rows, sl] = o[:tq]


def _xattn(q, mk, mv, *, rows_per_mem, n_mem, tm):
    n, xw = q.shape
    if tm <= rows_per_mem:
        groups, tiles_per_mem = 1, rows_per_mem // tm
        mem = lambda i: (i // tiles_per_mem, 0)
    else:
        groups = tm // rows_per_mem
        mem = lambda i: (i, 0)
    return pl.pallas_call(
        functools.partial(_xattn_kernel, scale=HEAD_DIM ** -0.5, heads=xw // HEAD_DIM, groups=groups),
        grid=(n // tm,),
        in_specs=[pl.BlockSpec((tm, xw), lambda i: (i, 0)),
                  pl.BlockSpec((groups * n_mem, xw), mem), pl.BlockSpec((groups * n_mem, xw), mem)],
        out_specs=pl.BlockSpec((tm, xw), lambda i: (i, 0)),
        out_shape=jax.ShapeDtypeStruct((n, xw), F32),
        compiler_params=_cparams("arbitrary"),
        name="xattn",
    )(q, mk, mv)


def _store_slabs(ref, x):
    m = x.shape[0]
    for s in range(SLAB_ROWS):
        ref[pl.ds(s, m, stride=SLAB_ROWS), :] = x[:, s * LANES:(s + 1) * LANES]


def _load_slabs(ref, m, first_row=0):
    return jnp.concatenate(
        [ref[pl.ds(first_row + s, m, stride=SLAB_ROWS), :] for s in range(SLAB_ROWS)], axis=1)


def _top_k_route(logits, counts, n_exp):
    tm = logits.shape[0]
    lane = lax.broadcasted_iota(jnp.int32, (tm, LANES), 1).astype(F32)
    lg = jnp.where(lane < n_exp, logits, NEG_INF)
    vals, ids, picks = [], [], []
    for _ in range(TOP_K):
        v = jnp.max(lg, axis=-1, keepdims=True)
        e = jnp.min(jnp.where(lg == v, lane, float(LANES)), axis=-1, keepdims=True)
        pick = lane == e
        lg = jnp.where(pick, NEG_INF, lg)
        vals.append(v), ids.append(e), picks.append(pick)
    exps = [jnp.exp(v - vals[0]) for v in vals]
    total = functools.reduce(lambda a, b: a + b, exps)
    chosen = functools.reduce(lambda a, b: a + b, [jnp.where(p, 1.0, 0.0) for p in picks])
    r = lax.broadcasted_iota(jnp.int32, (tm, tm), 0)
    c = lax.broadcasted_iota(jnp.int32, (tm, tm), 1)
    earlier = jnp.where(c < r, 1.0, 0.0).astype(BF16)
    before = _dot(earlier, chosen.astype(BF16)) + counts
    table = jnp.zeros((tm, LANES), F32)
    for k in range(TOP_K):
        rank = jnp.sum(jnp.where(picks[k], before, 0.0), axis=-1, keepdims=True)
        table = jnp.where(lane == k, exps[k] / total, table)
        table = jnp.where(lane == TOP_K + k, ids[k], table)
        table = jnp.where(lane == 2 * TOP_K + k, rank, table)
    return table, counts + jnp.sum(chosen, axis=0, keepdims=True)


def _xattn_out_kernel(x_ref, o_ref, wo_ref, g_ref, wr_ref, br_ref, cnt_ref,
                      x2_ref, xn_ref, rt_ref, cnt_out_ref, cnt_scr, *, n_exp):
    @pl.when(pl.program_id(0) == 0)
    def _():
        cnt_scr[...] = cnt_ref[...]

    x2 = x_ref[...] + _dot(o_ref[...].astype(BF16), wo_ref[...])
    x2_ref[...] = x2
    xn = _rms_f32(x2, g_ref[...])
    _store_slabs(xn_ref, xn)
    logits = _dot(xn.astype(BF16), wr_ref[...]) + br_ref[...]
    table, counts = _top_k_route(logits, cnt_scr[...], n_exp)
    rt_ref[...] = table
    cnt_scr[...] = counts
    cnt_out_ref[...] = counts


def _xattn_out(x, o, wo, g, wr, br, counts, *, n_exp, tm):
    n, d = x.shape
    assert d == SLAB_ROWS * LANES
    row = lambda i: (i, 0)
    fixed = lambda i: (0, 0)
    return pl.pallas_call(
        functools.partial(_xattn_out_kernel, n_exp=n_exp),
        grid=(n // tm,),
        in_specs=[pl.BlockSpec((tm, d), row), pl.BlockSpec((tm, o.shape[1]), row),
                  pl.BlockSpec(wo.shape, fixed), pl.BlockSpec((1, d), fixed),
                  pl.BlockSpec(wr.shape, fixed), pl.BlockSpec((1, LANES), fixed),
                  pl.BlockSpec((1, LANES), fixed)],
        out_specs=[pl.BlockSpec((tm, d), row), pl.BlockSpec((tm * SLAB_ROWS, LANES), row),
                   pl.BlockSpec((tm, LANES), row), pl.BlockSpec((1, LANES), fixed)],
        out_shape=[jax.ShapeDtypeStruct((n, d), F32), jax.ShapeDtypeStruct((n * SLAB_ROWS, LANES), F32),
                   jax.ShapeDtypeStruct((n, LANES), F32), jax.ShapeDtypeStruct((1, LANES), F32)],
        scratch_shapes=[pltpu.VMEM((1, LANES), F32)],
        compiler_params=_cparams("arbitrary"),
        name="xattn_out",
    )(x, o, wo, g, wr, br, counts)


def _slab(ref, row):
    return ref.at[pl.ds(pl.multiple_of(row * SLAB_ROWS, SLAB_ROWS), SLAB_ROWS)]


def _dispatch_kernel(dest_ref, x_ref, xs_in, xs_hbm, sem):
    del xs_in
    n_copies = dest_ref.shape[1]

    def slab_copy(i):
        return pltpu.make_async_copy(_slab(x_ref, i // TOP_K), _slab(xs_hbm, dest_ref[0, i]), sem)

    lax.fori_loop(0, n_copies, lambda i, c: (slab_copy(i).start(), c)[1], 0, unroll=8)
    lax.fori_loop(0, n_copies, lambda i, c: (slab_copy(i).wait(), c)[1], 0, unroll=8)


def _dispatch(xn_slabs, dest, xs, *, tc):
    n = xn_slabs.shape[0] // SLAB_ROWS
    return pl.pallas_call(
        _dispatch_kernel,
        grid=(n // tc,),
        in_specs=[pl.BlockSpec((None, 1, tc * TOP_K), lambda i: (i, 0, 0), memory_space=pltpu.SMEM),
                  pl.BlockSpec((tc * SLAB_ROWS, LANES), lambda i: (i, 0)),
                  pl.BlockSpec(memory_space=pl.ANY)],
        out_specs=pl.BlockSpec(memory_space=pl.ANY),
        out_shape=jax.ShapeDtypeStruct(xs.shape, xs.dtype),
        scratch_shapes=[pltpu.SemaphoreType.DMA(())],
        input_output_aliases={2: 0},
        compiler_params=_cparams("arbitrary"),
        name="moe_dispatch",
    )(dest, xn_slabs, xs)


def _gate_up_kernel(be_ref, nblk_ref, x_ref, wg_ref, wl_ref, bg_ref, bl_ref, h_ref):
    del be_ref

    used = pl.program_id(1) < nblk_ref[0]

    @pl.when(jnp.logical_not(used))
    def _():
        h_ref[...] = jnp.zeros_like(h_ref)

    @pl.when(used)
    def _():
        x = _load_slabs(x_ref, h_ref.shape[0]).astype(BF16)
        gate = jnp.minimum(_dot(x, wg_ref[...].astype(BF16)) + bg_ref[...], SWIGLU_LIMIT)
        lin = jnp.clip(_dot(x, wl_ref[...].astype(BF16)) + bl_ref[...], -SWIGLU_LIMIT, SWIGLU_LIMIT)
        h_ref[...] = ((lin + 1.0) * gate * jax.nn.sigmoid(SWIGLU_ALPHA * gate)).astype(h_ref.dtype)


def _gate_up(xs, w_gu, b_gu, blk_e, nblk, *, tm, tn):
    rows = xs.shape[0] // SLAB_ROWS
    n_exp, d, de2 = w_gu.shape
    de = de2 // 2
    nj = de // tn
    used = lambda j, r, be, nb: (jnp.minimum(r, nb[0] - 1), 0)
    exp_col = lambda off: (lambda j, r, be, nb: (be[jnp.minimum(r, nb[0] - 1)], 0, j + off))
    grid_spec = pltpu.PrefetchScalarGridSpec(
        num_scalar_prefetch=2,
        grid=(nj, rows // tm),
        in_specs=[pl.BlockSpec((tm * SLAB_ROWS, LANES), used),
                  pl.BlockSpec((None, d, tn), exp_col(0)), pl.BlockSpec((None, d, tn), exp_col(nj)),
                  pl.BlockSpec((None, 1, tn), exp_col(0)), pl.BlockSpec((None, 1, tn), exp_col(nj))],
        out_specs=pl.BlockSpec((tm, tn), lambda j, r, be, nb: (r, j)),
    )
    b3 = b_gu.reshape(n_exp, 1, de2)
    return pl.pallas_call(
        _gate_up_kernel,
        grid_spec=grid_spec,
        out_shape=jax.ShapeDtypeStruct((rows, de), BF16),
        compiler_params=_cparams("arbitrary", "arbitrary"),
        name="moe_gate_up",
    )(blk_e, nblk, xs, w_gu, w_gu, b3, b3)


def _down_kernel(be_ref, nblk_ref, h_ref, w_ref, b_ref, o_ref):
    del be_ref
    used = pl.program_id(0) < nblk_ref[0]

    @pl.when(jnp.logical_not(used))
    def _():
        o_ref[...] = jnp.zeros_like(o_ref)

    @pl.when(used)
    def _():
        out = _dot(h_ref[...], w_ref[...].astype(BF16)) + b_ref[...]
        _store_slabs(o_ref, out)


def _down(hdn, w_down, b_down, blk_e, nblk, *, tm):
    rows, de = hdn.shape
    n_exp, _, d = w_down.shape
    assert d == SLAB_ROWS * LANES
    used = lambda r, be, nb: (jnp.minimum(r, nb[0] - 1), 0)
    expert = lambda r, be, nb: (be[jnp.minimum(r, nb[0] - 1)], 0, 0)
    grid_spec = pltpu.PrefetchScalarGridSpec(
        num_scalar_prefetch=2,
        grid=(rows // tm,),
        in_specs=[pl.BlockSpec((tm, de), used),
                  pl.BlockSpec((None, de, d), expert), pl.BlockSpec((None, 1, d), expert)],
        out_specs=pl.BlockSpec((tm * SLAB_ROWS, LANES), lambda r, be, nb: (r, 0)),
    )
    return pl.pallas_call(
        _down_kernel,
        grid_spec=grid_spec,
        out_shape=jax.ShapeDtypeStruct((rows * SLAB_ROWS, LANES), F32),
        compiler_params=_cparams("arbitrary"),
        name="moe_down",
    )(blk_e, nblk, hdn, w_down, b_down.reshape(n_exp, 1, d))


def _combine_kernel(pos_ref, x_ref, gate_ref, rows_hbm, g_ref, y_ref, buf, sem):
    tc = x_ref.shape[0]
    hc = tc // 2
    per_half = TOP_K * hc

    def slab_copy(half, i):
        return pltpu.make_async_copy(_slab(rows_hbm, pos_ref[0, half * per_half + i]),
                                     _slab(buf.at[half], i), sem.at[half])

    for half in range(2):
        lax.fori_loop(0, per_half, lambda i, c, half=half: (slab_copy(half, i).start(), c)[1], 0, unroll=8)
    for half in range(2):
        lax.fori_loop(0, per_half, lambda i, c, half=half: (slab_copy(half, i).wait(), c)[1], 0, unroll=8)
        rows = slice(half * hc, (half + 1) * hc)
        x3 = x_ref[rows, :]
        for k in range(TOP_K):
            x3 = x3 + gate_ref[rows, k:k + 1] * _load_slabs(buf.at[half], hc, first_row=k * hc * SLAB_ROWS)
        y_ref[rows, :] = _rms_f32(x3, g_ref[...])


def _combine(x, gates, rows, pos, g, *, tc):
    n, d = x.shape
    return pl.pallas_call(
        _combine_kernel,
        grid=(n // tc,),
        in_specs=[pl.BlockSpec((None, 1, TOP_K * tc), lambda i: (i, 0, 0), memory_space=pltpu.SMEM),
                  pl.BlockSpec((tc, d), lambda i: (i, 0)),
                  pl.BlockSpec((tc, TOP_K), lambda i: (i, 0)),
                  pl.BlockSpec(memory_space=pl.ANY),
                  pl.BlockSpec((1, d), lambda i: (0, 0))],
        out_specs=pl.BlockSpec((tc, d), lambda i: (i, 0)),
        out_shape=jax.ShapeDtypeStruct((n, d), F32),
        scratch_shapes=[pltpu.VMEM((2, TOP_K * (tc // 2) * SLAB_ROWS, LANES), F32),
                        pltpu.SemaphoreType.DMA((2,))],
        compiler_params=_cparams("arbitrary"),
        name="moe_combine",
    )(pos, x, gates, rows, g)


def _row_layout(table, counts, n_exp, tm):
    n_tok = table.shape[0]
    gates = table[:, :TOP_K]
    expert = table[:, TOP_K:2 * TOP_K].astype(jnp.int32)
    rank = table[:, 2 * TOP_K:3 * TOP_K].astype(jnp.int32)
    counts = counts.astype(jnp.int32)
    padded = (counts + tm - 1) // tm * tm
    pad_ends = jnp.cumsum(padded)
    pad_starts = pad_ends - padded
    is_e = expert[:, :, None] == jnp.arange(n_exp, dtype=jnp.int32)[None, None, :]
    dest = rank + jnp.sum(jnp.where(is_e, pad_starts[None, None, :], 0), axis=-1)
    n_blocks = -(-n_tok * TOP_K // tm) + n_exp
    tile_start = jnp.arange(n_blocks, dtype=jnp.int32) * tm
    blk_e = jnp.minimum(jnp.sum((pad_ends[None, :] <= tile_start[:, None]).astype(jnp.int32), axis=1),
                        n_exp - 1)
    nblk = (pad_ends[-1] // tm).reshape(1)
    return gates, dest, blk_e, nblk, n_blocks


def _moe_final(x2_groups, xn_groups, table_groups, counts, w_gu, b_gu, w_down, b_down, g_final):
    n_exp = w_gu.shape[0]
    tm = min(EXPERT_TILE, TOP_K * sum(x.shape[0] for x in x2_groups))
    gates, dest, blk_e, nblk, n_blocks = _row_layout(
        jnp.concatenate(table_groups, axis=0), counts[0, :n_exp], n_exp, tm)
    xs = jnp.zeros((n_blocks * tm * SLAB_ROWS, LANES), F32)
    start = 0
    for xn in xn_groups:
        n = xn.shape[0] // SLAB_ROWS
        tc = min(COMBINE_TILE, n)
        xs = _dispatch(xn, dest[start:start + n].reshape(n // tc, 1, tc * TOP_K), xs, tc=tc)
        start += n
    hdn = _gate_up(xs, w_gu, b_gu, blk_e, nblk, tm=tm, tn=min(1024, w_gu.shape[2] // 2))
    rows = _down(hdn, w_down, b_down, blk_e, nblk, tm=tm)
    outs, start = [], 0
    for x2 in x2_groups:
        n = x2.shape[0]
        tc = min(COMBINE_TILE, n)
        pos = dest[start:start + n].reshape(n // tc, 2, tc // 2, TOP_K).transpose(0, 1, 3, 2)
        outs.append(_combine(x2, gates[start:start + n], rows, pos.reshape(n // tc, 1, TOP_K * tc),
                             g_final, tc=tc))
        start += n
    return outs


def _layer_common(x, attn, pool, wa, wp, g_x, wq, mk, mv, rows_per_mem, xattn_tile,
                  wo, g_f, wr, br, counts, n_exp):
    n = x.shape[0]
    x1, qx = _mix_out(x, attn, pool, wa, wp, g_x, wq, tm=min(256, n))
    o = _xattn(qx, mk, mv, rows_per_mem=rows_per_mem, n_mem=mk.shape[0] * rows_per_mem // n,
               tm=xattn_tile)
    return _xattn_out(x1, o, wo, g_f, wr, br, counts, n_exp=n_exp, tm=min(256, n))


def kernel(x_prompt, x_sample, mem_prompt, cache_k, cache_v, cache_logf, cache_mem_k, cache_mem_v,
           state_pool, page_table, g_mix, w_in, b_forget, w_pool, s_pool, w_out, g_xattn, g_mem,
           w_xq, w_xk, w_xv, w_xo, g_ffn, w_router, b_router, w_gu, b_gu, w_down, b_down, g_final):
    depth = w_in.shape[0]
    bp, tp, d = x_prompt.shape
    bs, ts, _ = x_sample.shape
    n_pool, page, heads, hd = cache_k.shape[1:]
    fw = heads * hd
    pw = state_pool.shape[-1]
    n_mem = mem_prompt.shape[1]
    xw = w_xq.shape[-1]
    n_exp = w_router.shape[-1]
    past_len = page_table.shape[1] * page
    assert hd == HEAD_DIM and page == PAGE_SIZE and heads == 8 and pw == fw
    assert w_in.shape[-1] == 3 * fw + heads + pw and state_pool.shape[2] == POOL_STATE
    assert depth == 1, "the experts of all groups are evaluated together after the last layer"

    xp = x_prompt.reshape(bp * tp, d)
    xs = x_sample.reshape(bs * ts, d)
    row = lambda a: a.reshape(1, -1)
    outs = {k: [] for k in ("kp", "vp", "lfp", "pp", "mkp", "mvp", "ks", "vs", "lfs", "ps")}
    for l in range(depth):
        w = w_in[l]
        w4 = jnp.concatenate([w[:, :3 * fw], w[:, 3 * fw + heads:]], axis=1).astype(BF16)
        wf = jnp.pad(w[:, 3 * fw:3 * fw + heads], ((0, 0), (0, LANES - heads))).astype(BF16)
        bfp = jnp.pad(b_forget[l], (0, LANES - heads)).reshape(1, LANES)
        wpool = w_pool[l].astype(BF16)
        spool = row(s_pool[l])
        wa = w_out[l, :fw].astype(BF16)
        wp = w_out[l, fw:].astype(BF16)
        wq = w_xq[l].astype(BF16)
        wkv = jnp.concatenate([w_xk[l], w_xv[l]], axis=1).astype(BF16)
        wo = w_xo[l].astype(BF16)
        wr = jnp.pad(w_router[l], ((0, 0), (0, LANES - n_exp))).astype(BF16)
        br = jnp.pad(b_router[l], (0, LANES - n_exp)).reshape(1, LANES)

        tm = min(512, tp)
        q, k, v, u, kb, vb, lf, ct = _in_proj(xp, row(g_mix[l]), w4, wf, bfp, seq_len=tp, heads=heads, tm=tm)
        attn = _fox_prompt(q, kb, vb, ct, batch=bp, seq_len=tp, heads=heads, tq=tm,
                           heads_per_step=FLASH_HEADS_PER_STEP)
        pool = _pool_prompt(u, wpool, spool, seq_len=tp, tm=tm)
        outs["kp"].append(k.reshape(bp, tp, heads, hd))
        outs["vp"].append(v.reshape(bp, tp, heads, hd))
        outs["lfp"].append(lf.reshape(bp, tp, heads))
        outs["pp"].append(u.reshape(bp, tp, pw)[:, tp - POOL_STATE:])
        mk, mv = _mem_kv(mem_prompt.reshape(bp * n_mem, d), row(g_mem[l]), wkv, tm=min(256, bp * n_mem))
        outs["mkp"].append(mk.reshape(bp, n_mem, xw // HEAD_DIM, HEAD_DIM))
        outs["mvp"].append(mv.reshape(bp, n_mem, xw // HEAD_DIM, HEAD_DIM))
        x2p, xnp_, rtp, counts = _layer_common(xp, attn, pool, wa, wp, row(g_xattn[l]), wq, mk, mv, tp, tm,
                                               wo, row(g_ffn[l]), wr, br, jnp.zeros((1, LANES), F32), n_exp)

        ns = bs * ts
        tms = min(512, ns)
        q, k, v, u, _, _, lf, ct = _in_proj(xs, row(g_mix[l]), w4, wf, bfp, seq_len=ts, heads=heads, tm=tms)
        c_new = ct.reshape(ns // tms, heads, tms // ts, ts).transpose(0, 2, 3, 1).reshape(bs, 1, ts * heads)
        c_new = jnp.pad(c_new, ((0, 0), (0, 0), (0, LANES - ts * heads)))
        scan = _logf_scan(cache_logf[l].reshape(n_pool, page * heads), heads=heads,
                          tm=_divisor_tile(n_pool, 512))
        attn = _fox_decode(q, k.reshape(bs, ts * heads, hd), v.reshape(bs, ts * heads, hd), c_new,
                           cache_k[l].reshape(n_pool, page * heads, hd),
                           cache_v[l].reshape(n_pool, page * heads, hd),
                           scan, page_table, heads=heads, n_new=ts,
                           pages_per_step=_divisor_tile(page_table.shape[1], DECODE_PAGES_PER_STEP, 1))
        u3 = u.reshape(bs, ts, pw)
        pool = _pool_sample(u3.transpose(1, 0, 2), state_pool[l].transpose(1, 0, 2), wpool, spool,
                            past_len=past_len)
        pool = pool.transpose(1, 0, 2).reshape(ns, pw)
        outs["ks"].append(k.reshape(bs, ts, heads, hd))
        outs["vs"].append(v.reshape(bs, ts, heads, hd))
        outs["lfs"].append(lf.reshape(bs, ts, heads))
        outs["ps"].append(jnp.concatenate([state_pool[l], u3], axis=1)[:, ts:])
        mks = cache_mem_k[l].reshape(bs * n_mem, xw)
        mvs = cache_mem_v[l].reshape(bs * n_mem, xw)
        x2s, xns, rts, counts = _layer_common(xs, attn, pool, wa, wp, row(g_xattn[l]), wq, mks, mvs, ts,
                                              ts * _divisor_tile(bs, XATTN_SEQS_PER_STEP, 1),
                                              wo, row(g_ffn[l]), wr, br, counts, n_exp)

        yp, ys = _moe_final([x2p, x2s], [xnp_, xns], [rtp, rts], counts, w_gu[l], b_gu[l], w_down[l],
                            b_down[l], row(g_final))

    st = lambda name: jnp.stack(outs[name])
    return (yp.reshape(bp, tp, d), ys.reshape(bs, ts, d),
            st("kp"), st("vp"), st("lfp"), st("pp"), st("mkp"), st("mvp"),
            st("ks"), st("vs"), st("lfs"), st("ps"))
```

```python
import functools

import jax
import jax.numpy as jnp
from jax import lax
from jax.experimental import pallas as pl
from jax.experimental.pallas import tpu as pltpu

F32 = jnp.float32
BF16 = jnp.bfloat16

HEAD_DIM = 128
PAGE_SIZE = 128
POOL_WINDOWS = (2, 4, 8, 16)
POOL_STATE = max(POOL_WINDOWS) - 1
POOL_HALO = 16
TOP_K = 4
SWIGLU_ALPHA = 1.702
SWIGLU_LIMIT = 7.0
NORM_EPS = 1e-5
LANES = 128
BF16_ROWS = 16
SLAB_ROWS = 16
SCAN_ROWS = 8
VMEM_LIMIT = 56 * 1024 * 1024
NEG_INF = float("-inf")
EXPERT_TILE = 512
ROW_VARIANTS = 4
COMBINE_TILE = 128


DECODE_PAGES_PER_STEP = 8
FLASH_HEADS_PER_STEP = 4
XATTN_SEQS_PER_STEP = 8


def _divisor_tile(n, cap, multiple=8):
    for t in range(min(cap, n), 0, -1):
        if n % t == 0 and t % multiple == 0:
            return t
    raise ValueError(f"no tile for {n}")


def _cparams(*sem):
    return pltpu.CompilerParams(dimension_semantics=sem, vmem_limit_bytes=VMEM_LIMIT)


def _rms_f32(x, g):
    return x * lax.rsqrt(jnp.mean(x * x, axis=-1, keepdims=True) + NORM_EPS) * g


def _dot(a, b):
    return jnp.dot(a, b, preferred_element_type=F32)


def _dot_nt(a, b):
    return lax.dot_general(a, b, (((1,), (1,)), ((), ())), preferred_element_type=F32)


def _pad_rows(x):
    rows = x.shape[0]
    if rows >= BF16_ROWS:
        return x
    return jnp.concatenate([x, jnp.zeros((BF16_ROWS - rows,) + x.shape[1:], x.dtype)], axis=0)


def _split3(x):
    hi = x.astype(BF16)
    r = x - hi.astype(F32)
    mid = r.astype(BF16)
    lo = (r - mid.astype(F32)).astype(BF16)
    return hi, mid, lo


def _lane_cumsum(x, seg):
    n = x.shape[-1]
    r = lax.broadcasted_iota(jnp.int32, (n, n), 0)
    c = lax.broadcasted_iota(jnp.int32, (n, n), 1)
    keep = r <= c
    if seg < n:
        keep = jnp.logical_and(keep, (r // seg) == (c // seg))
    tri = jnp.where(keep, 1.0, 0.0).astype(BF16)
    rows = x.shape[0]
    hi, mid, lo = _split3(_pad_rows(x))
    return (_dot(hi, tri) + _dot(mid, tri) + _dot(lo, tri))[:rows]


def _log_sigmoid(z):
    return jnp.minimum(z, 0.0) - jnp.log1p(jnp.exp(-jnp.abs(z)))


def _in_proj_kernel(x_ref, g_ref, w_ref, wf_ref, bf_ref,
                    q_ref, k_ref, v_ref, u_ref, kb_ref, vb_ref, lf_ref, ct_ref,
                    xn_scr, carry_scr, *, tiles_per_seq, seg, heads):
    i = pl.program_id(0)
    j = pl.program_id(1)

    @pl.when(j == 0)
    def _():
        xn = _rms_f32(x_ref[...], g_ref[...]).astype(BF16)
        xn_scr[...] = xn
        lf = _log_sigmoid(_dot(xn, wf_ref[...]) + bf_ref[...])
        lf_ref[...] = lf[:, :heads]
        lft = lf.T[:8]
        c = _lane_cumsum(lft, seg)

        @pl.when(i % tiles_per_seq == 0)
        def _():
            carry_scr[...] = jnp.zeros_like(carry_scr)

        c = c + carry_scr[:, :1]
        ct_ref[0] = c
        carry_scr[...] = jnp.broadcast_to(c[:, -1:], carry_scr.shape)

    y = _dot(xn_scr[...], w_ref[...])

    @pl.when(j == 0)
    def _():
        q_ref[...] = y

    @pl.when(j == 1)
    def _():
        k_ref[...] = y
        kb_ref[...] = y.astype(BF16)

    @pl.when(j == 2)
    def _():
        v_ref[...] = y
        vb_ref[...] = y.astype(BF16)

    @pl.when(j == 3)
    def _():
        u_ref[...] = y


def _in_proj(x, g, w4, wf, bfp, *, seq_len, heads, tm):
    n, d = x.shape
    width = w4.shape[1] // 4
    if seq_len >= tm:
        tiles_per_seq, seg = seq_len // tm, tm
    else:
        tiles_per_seq, seg = 1, seq_len
    n_tiles = n // tm
    row = lambda i, j: (i, 0)
    big = jax.ShapeDtypeStruct((n, width), F32)
    bigb = jax.ShapeDtypeStruct((n, width), BF16)
    return pl.pallas_call(
        functools.partial(_in_proj_kernel, tiles_per_seq=tiles_per_seq, seg=seg, heads=heads),
        grid=(n_tiles, 4),
        in_specs=[
            pl.BlockSpec((tm, d), row),
            pl.BlockSpec((1, d), lambda i, j: (0, 0)),
            pl.BlockSpec((d, width), lambda i, j: (0, j)),
            pl.BlockSpec((d, LANES), lambda i, j: (0, 0)),
            pl.BlockSpec((1, LANES), lambda i, j: (0, 0)),
        ],
        out_specs=[
            pl.BlockSpec((tm, width), row), pl.BlockSpec((tm, width), row),
            pl.BlockSpec((tm, width), row), pl.BlockSpec((tm, width), row),
            pl.BlockSpec((tm, width), row), pl.BlockSpec((tm, width), row),
            pl.BlockSpec((tm, heads), row),
            pl.BlockSpec((1, 8, tm), lambda i, j: (i, 0, 0)),
        ],
        out_shape=[big, big, big, big, bigb, bigb,
                   jax.ShapeDtypeStruct((n, heads), F32),
                   jax.ShapeDtypeStruct((n_tiles, 8, tm), F32)],
        scratch_shapes=[pltpu.VMEM((tm, d), BF16), pltpu.VMEM((8, LANES), F32)],
        compiler_params=_cparams("arbitrary", "arbitrary"),
        name="in_proj",
    )(x, g, w4, wf, bfp)


def _fox_flash_kernel(qi_ref, ki_ref, q_ref, k_ref, v_ref, ct_ref, o_ref, m_scr, l_scr, acc_scr,
                      *, scale, heads_per_step):
    hp = pl.program_id(1)
    t = pl.program_id(2)
    qi = qi_ref[t]
    ki = ki_ref[t]

    @pl.when(ki == 0)
    def _():
        m_scr[...] = jnp.full_like(m_scr, NEG_INF)
        l_scr[...] = jnp.zeros_like(l_scr)
        acc_scr[...] = jnp.zeros_like(acc_scr)

    def step(diagonal):
        for hh in range(heads_per_step):
            sl = slice(hh * HEAD_DIM, (hh + 1) * HEAD_DIM)
            q = (q_ref[:, sl] * scale).astype(BF16)
            s = _dot_nt(q, k_ref[:, sl]) - ct_ref[0, pl.ds(hp * heads_per_step + hh, 1), :]
            if diagonal:
                r = lax.broadcasted_iota(jnp.int32, s.shape, 0)
                c = lax.broadcasted_iota(jnp.int32, s.shape, 1)
                s = jnp.where(c <= r, s, NEG_INF)
            m_prev = m_scr[hh]
            m_new = jnp.maximum(m_prev, jnp.max(s, axis=-1, keepdims=True))
            alpha = jnp.exp(m_prev - m_new)
            p = jnp.exp(s - m_new[:, :1])
            l_new = alpha * l_scr[hh] + jnp.sum(p, axis=-1, keepdims=True)
            acc = alpha * acc_scr[hh] + _dot(p.astype(BF16), v_ref[:, sl])
            if diagonal:
                o_ref[:, sl] = (acc / l_new).astype(o_ref.dtype)
            else:
                l_scr[hh] = l_new
                acc_scr[hh] = acc
                m_scr[hh] = m_new

    @pl.when(ki < qi)
    def _():
        step(False)

    @pl.when(ki == qi)
    def _():
        step(True)


def _fox_prompt(q, kb, vb, ct, *, batch, seq_len, heads, tq, heads_per_step):
    n, width = q.shape
    nq = seq_len // tq
    hw = heads_per_step * HEAD_DIM
    pairs = [(qi, ki) for qi in range(nq) for ki in range(qi + 1)]
    qi_tab = jnp.array([p[0] for p in pairs], jnp.int32)
    ki_tab = jnp.array([p[1] for p in pairs], jnp.int32)
    q_blk = lambda b, hp, t, qt, kt: (b * nq + qt[t], hp)
    kv_blk = lambda b, hp, t, qt, kt: (b * nq + kt[t], hp)
    grid_spec = pltpu.PrefetchScalarGridSpec(
        num_scalar_prefetch=2,
        grid=(batch, heads // heads_per_step, len(pairs)),
        in_specs=[
            pl.BlockSpec((tq, hw), q_blk),
            pl.BlockSpec((tq, hw), kv_blk),
            pl.BlockSpec((tq, hw), kv_blk),
            pl.BlockSpec((1, 8, tq), lambda b, hp, t, qt, kt: (b * nq + kt[t], 0, 0)),
        ],
        out_specs=pl.BlockSpec((tq, hw), q_blk),
        scratch_shapes=[pltpu.VMEM((heads_per_step, tq, LANES), F32),
                        pltpu.VMEM((heads_per_step, tq, LANES), F32),
                        pltpu.VMEM((heads_per_step, tq, HEAD_DIM), F32)],
    )
    return pl.pallas_call(
        functools.partial(_fox_flash_kernel, scale=HEAD_DIM ** -0.5, heads_per_step=heads_per_step),
        grid_spec=grid_spec,
        out_shape=jax.ShapeDtypeStruct((n, width), BF16),
        compiler_params=_cparams("arbitrary", "arbitrary", "arbitrary"),
        name="fox_prompt",
    )(qi_tab, ki_tab, q, kb, vb, ct)


def _logf_scan_kernel(x_ref, o_ref, *, heads):
    n = x_ref.shape[1]
    r = lax.broadcasted_iota(jnp.int32, (n, n), 0)
    c = lax.broadcasted_iota(jnp.int32, (n, n), 1)
    same_head = jnp.bitwise_and(r, heads - 1) == jnp.bitwise_and(c, heads - 1)
    m_tot = jnp.where(same_head, 1.0, 0.0).astype(BF16)
    m_cum = jnp.where(jnp.logical_and(same_head, r <= c), 1.0, 0.0).astype(BF16)
    hi, mid, lo = _split3(x_ref[...])
    o_ref[:, :n] = _dot(hi, m_cum) + _dot(mid, m_cum) + _dot(lo, m_cum)
    o_ref[:, n:] = _dot(hi, m_tot) + _dot(mid, m_tot) + _dot(lo, m_tot)


def _logf_scan(logf_pages, *, heads, tm):
    n_pool, n = logf_pages.shape
    return pl.pallas_call(
        functools.partial(_logf_scan_kernel, heads=heads),
        grid=(n_pool // tm,),
        in_specs=[pl.BlockSpec((tm, n), lambda i: (i, 0))],
        out_specs=pl.BlockSpec((tm, 2 * n), lambda i: (i, 0)),
        out_shape=jax.ShapeDtypeStruct((n_pool, 2 * n), F32),
        compiler_params=_cparams("arbitrary"),
        name="logf_scan",
    )(logf_pages)


def _fox_decode_kernel(pt_ref, q_ref, *refs, scale, heads, n_new, pages_per_step):
    pps = pages_per_step
    k_refs, v_refs, scan_refs = refs[:pps], refs[pps:2 * pps], refs[2 * pps:3 * pps]
    kn_ref, vn_ref, cn_ref, o_ref, m_scr, l_scr, acc_scr, carry_scr = refs[3 * pps:]
    step = pl.program_id(1)
    last = pl.num_programs(1) - 1
    rows = heads * n_new
    page_rows = PAGE_SIZE * heads

    @pl.when(step == 0)
    def _():
        m_scr[...] = jnp.full_like(m_scr, NEG_INF)
        l_scr[...] = jnp.zeros_like(l_scr)
        acc_scr[...] = jnp.zeros_like(acc_scr)
        carry_scr[...] = jnp.zeros_like(carry_scr)

    qb = jnp.concatenate([q_ref[:, h * HEAD_DIM:(h + 1) * HEAD_DIM] for h in range(heads)], axis=0)
    qb = (qb * scale).astype(BF16)

    def update(s, pv):
        m_prev = m_scr[...]
        m_new = jnp.maximum(m_prev, jnp.max(s, axis=-1, keepdims=True))
        alpha = jnp.exp(m_prev - m_new)
        p = jnp.exp(s - m_new[:, :1])
        l_scr[...] = alpha * l_scr[...] + jnp.sum(p, axis=-1, keepdims=True)
        acc_scr[...] = alpha * acc_scr[...] + pv(p.astype(BF16))
        m_scr[...] = m_new

    r = lax.broadcasted_iota(jnp.int32, (rows, page_rows), 0)
    c = lax.broadcasted_iota(jnp.int32, (rows, page_rows), 1)
    head_ok = jnp.bitwise_and(c, heads - 1) == r // n_new
    carry = carry_scr[...]
    parts = []
    for i in range(pps):
        s = _dot_nt(qb, k_refs[i][0].astype(BF16))
        page = pt_ref[(pl.program_id(0) * pl.num_programs(1) + step) * pps + i]
        scan = scan_refs[i][pl.ds(page % SCAN_ROWS, 1), :]
        parts.append(jnp.where(head_ok, s - (scan[:, :page_rows] + carry), NEG_INF))
        carry = carry + scan[:, page_rows:]
    carry_scr[...] = carry

    def page_values(p):
        out = _dot(p[:, :page_rows], v_refs[0][0].astype(BF16))
        for i in range(1, pps):
            out = out + _dot(p[:, i * page_rows:(i + 1) * page_rows], v_refs[i][0].astype(BF16))
        return out

    update(jnp.concatenate(parts, axis=1), page_values)

    @pl.when(step == last)
    def _():
        pad = jnp.zeros((LANES - rows, HEAD_DIM), F32)
        rn = lax.broadcasted_iota(jnp.int32, (rows, LANES), 0)
        cn = lax.broadcasted_iota(jnp.int32, (rows, LANES), 1)
        ok = jnp.logical_and(jnp.bitwise_and(cn, heads - 1) == rn // n_new,
                             cn // heads <= jnp.bitwise_and(rn, n_new - 1))
        s = _dot_nt(qb, jnp.concatenate([kn_ref[0], pad], axis=0).astype(BF16))
        s = jnp.where(ok, s - (cn_ref[0] + carry[:, :LANES]), NEG_INF)
        vn = jnp.concatenate([vn_ref[0], pad], axis=0).astype(BF16)
        update(s, lambda p: _dot(p, vn))
        out = acc_scr[...] / l_scr[...]
        for h in range(heads):
            o_ref[:, h * HEAD_DIM:(h + 1) * HEAD_DIM] = out[h * n_new:(h + 1) * n_new]


def _fox_decode(q, k_new, v_new, c_new, kc, vc, scan, page_table, *, heads, n_new, pages_per_step):
    n, width = q.shape
    n_seq, n_pages = page_table.shape
    pps = pages_per_step
    page_rows = PAGE_SIZE * heads
    seq = lambda b, p, pt: (b, 0, 0)
    page = lambda i: (lambda b, p, pt: (pt[b * n_pages + p * pps + i], 0, 0))
    kv_specs = [pl.BlockSpec((1, page_rows, HEAD_DIM), page(i)) for i in range(pps)]
    scan_page = lambda i: (lambda b, p, pt: (pt[b * n_pages + p * pps + i] // SCAN_ROWS, 0))
    scan_specs = [pl.BlockSpec((SCAN_ROWS, 2 * page_rows), scan_page(i)) for i in range(pps)]
    grid_spec = pltpu.PrefetchScalarGridSpec(
        num_scalar_prefetch=1,
        grid=(n_seq, n_pages // pps),
        in_specs=[pl.BlockSpec((n_new, width), lambda b, p, pt: (b, 0))]
        + kv_specs + kv_specs + scan_specs
        + [pl.BlockSpec((1, n_new * heads, HEAD_DIM), seq), pl.BlockSpec((1, n_new * heads, HEAD_DIM), seq),
           pl.BlockSpec((1, 1, LANES), seq)],
        out_specs=pl.BlockSpec((n_new, width), lambda b, p, pt: (b, 0)),
        scratch_shapes=[pltpu.VMEM((heads * n_new, LANES), F32), pltpu.VMEM((heads * n_new, LANES), F32),
                        pltpu.VMEM((heads * n_new, HEAD_DIM), F32), pltpu.VMEM((1, page_rows), F32)],
    )
    return pl.pallas_call(
        functools.partial(_fox_decode_kernel, scale=HEAD_DIM ** -0.5, heads=heads, n_new=n_new,
                          pages_per_step=pps),
        grid_spec=grid_spec,
        out_shape=jax.ShapeDtypeStruct((n, width), F32),
        compiler_params=_cparams("arbitrary", "arbitrary"),
        name="fox_decode",
    )(page_table.reshape(-1), q, *([kc] * pps), *([vc] * pps), *([scan] * pps), k_new, v_new, c_new)


def _pool_prompt_kernel(u_ref, halo_ref, w_ref, s_ref, o_ref, *, tiles_per_seq, gd):
    tm = u_ref.shape[0]
    tile_in_seq = pl.program_id(0) % tiles_per_seq
    pos = tile_in_seq * tm + lax.broadcasted_iota(jnp.int32, (tm, 1), 0)
    for g, w in enumerate(POOL_WINDOWS):
        sl = slice(g * gd, (g + 1) * gd)
        u = u_ref[:, sl]
        halo = jnp.where(tile_in_seq > 0, halo_ref[:, sl], 0.0)
        acc = jnp.concatenate([halo, u], axis=0)
        k = 1
        while k < w:
            acc = acc + pltpu.roll(acc, k, axis=0)
            k *= 2
        cnt = jnp.minimum(pos + 1, w).astype(F32)
        d = acc[POOL_HALO:] / cnt - u
        o_ref[:, sl] = (_dot(d.astype(BF16), w_ref[g]) * s_ref[:, sl]).astype(o_ref.dtype)


def _pool_prompt(u, w_pool, s_pool, *, seq_len, tm):
    n, pw = u.shape
    gd = pw // len(POOL_WINDOWS)
    halo_blocks = tm // POOL_HALO
    return pl.pallas_call(
        functools.partial(_pool_prompt_kernel, tiles_per_seq=seq_len // tm, gd=gd),
        grid=(n // tm,),
        in_specs=[
            pl.BlockSpec((tm, pw), lambda i: (i, 0)),
            pl.BlockSpec((POOL_HALO, pw), lambda i: (jnp.maximum(i * halo_blocks - 1, 0), 0)),
            pl.BlockSpec(w_pool.shape, lambda i: (0, 0, 0)),
            pl.BlockSpec((1, pw), lambda i: (0, 0)),
        ],
        out_specs=pl.BlockSpec((tm, pw), lambda i: (i, 0)),
        out_shape=jax.ShapeDtypeStruct((n, pw), BF16),
        compiler_params=_cparams("arbitrary"),
        name="pool_prompt",
    )(u, u, w_pool, s_pool)


def _pool_sample_kernel(u_ref, st_ref, w_ref, s_ref, o_ref, *, gd, past_len):
    n_new = u_ref.shape[0]

    def ext(j, sl):
        return st_ref[j, :, sl] if j < POOL_STATE else u_ref[j - POOL_STATE, :, sl]

    for g, w in enumerate(POOL_WINDOWS):
        sl = slice(g * gd, (g + 1) * gd)
        for t in range(n_new):
            cur = ext(POOL_STATE + t, sl)
            acc = cur
            for j in range(1, w):
                acc = acc + ext(POOL_STATE + t - j, sl)
            d = acc / float(min(past_len + t + 1, w)) - cur
            o_ref[t, :, sl] = (_dot(d.astype(BF16), w_ref[g]) * s_ref[:, sl]).astype(o_ref.dtype)


def _pool_sample(u_t, st_t, w_pool, s_pool, *, past_len):
    n_new, n_seq, pw = u_t.shape
    gd = pw // len(POOL_WINDOWS)
    return pl.pallas_call(
        functools.partial(_pool_sample_kernel, gd=gd, past_len=past_len),
        out_shape=jax.ShapeDtypeStruct((n_new, n_seq, pw), BF16),
        compiler_params=pltpu.CompilerParams(vmem_limit_bytes=VMEM_LIMIT),
        name="pool_sample",
    )(u_t, st_t, w_pool, s_pool)


def _mix_out_kernel(x_ref, a_ref, p_ref, wa_ref, wp_ref, g_ref, wq_ref, x1_ref, qx_ref):
    x1 = x_ref[...] + _dot(a_ref[...].astype(BF16), wa_ref[...]) + _dot(p_ref[...], wp_ref[...])
    x1_ref[...] = x1
    qx_ref[...] = _dot(_rms_f32(x1, g_ref[...]).astype(BF16), wq_ref[...])


def _mix_out(x, attn, pool, wa, wp, g, wq, *, tm):
    n, d = x.shape
    xw = wq.shape[1]
    row = lambda i: (i, 0)
    fixed = lambda i: (0, 0)
    return pl.pallas_call(
        _mix_out_kernel,
        grid=(n // tm,),
        in_specs=[
            pl.BlockSpec((tm, d), row), pl.BlockSpec((tm, attn.shape[1]), row),
            pl.BlockSpec((tm, pool.shape[1]), row),
            pl.BlockSpec(wa.shape, fixed), pl.BlockSpec(wp.shape, fixed),
            pl.BlockSpec((1, d), fixed), pl.BlockSpec(wq.shape, fixed),
        ],
        out_specs=[pl.BlockSpec((tm, d), row), pl.BlockSpec((tm, xw), row)],
        out_shape=[jax.ShapeDtypeStruct((n, d), F32), jax.ShapeDtypeStruct((n, xw), F32)],
        compiler_params=_cparams("arbitrary"),
        name="mix_out",
    )(x, attn, pool, wa, wp, g, wq)


def _mem_kv_kernel(m_ref, g_ref, w_ref, k_ref, v_ref):
    y = _dot(_rms_f32(m_ref[...], g_ref[...]).astype(BF16), w_ref[...])
    xw = k_ref.shape[1]
    k_ref[...] = y[:, :xw]
    v_ref[...] = y[:, xw:]


def _mem_kv(mem, g, wkv, *, tm):
    n, d = mem.shape
    xw = wkv.shape[1] // 2
    out = jax.ShapeDtypeStruct((n, xw), F32)
    return pl.pallas_call(
        _mem_kv_kernel,
        grid=(n // tm,),
        in_specs=[pl.BlockSpec((tm, d), lambda i: (i, 0)), pl.BlockSpec((1, d), lambda i: (0, 0)),
                  pl.BlockSpec(wkv.shape, lambda i: (0, 0))],
        out_specs=[pl.BlockSpec((tm, xw), lambda i: (i, 0))] * 2,
        out_shape=[out, out],
        compiler_params=_cparams("arbitrary"),
        name="mem_kv",
    )(mem, g, wkv)


def _xattn_kernel(q_ref, k_ref, v_ref, o_ref, *, scale, heads, groups):
    tq = q_ref.shape[0] // groups
    n_mem = k_ref.shape[0] // groups
    for g in range(groups):
        rows = slice(g * tq, (g + 1) * tq)
        mem = slice(g * n_mem, (g + 1) * n_mem)
        for h in range(heads):
            sl = slice(h * HEAD_DIM, (h + 1) * HEAD_DIM)
            q = _pad_rows(q_ref[rows, sl] * scale).astype(BF16)
            s = _dot_nt(q, k_ref[mem, sl].astype(BF16))
            p = jnp.exp(s - jnp.max(s, axis=-1, keepdims=True))
            o = _dot(p.astype(BF16), v_ref[mem, sl].astype(BF16)) / jnp.sum(p, axis=-1, keepdims=True)
            o_ref[rows, sl] = o[:tq]


def _xattn(q, mk, mv, *, rows_per_mem, n_mem, tm):
    n, xw = q.shape
    if tm <= rows_per_mem:
        groups, tiles_per_mem = 1, rows_per_mem // tm
        mem = lambda i: (i // tiles_per_mem, 0)
    else:
        groups = tm // rows_per_mem
        mem = lambda i: (i, 0)
    return pl.pallas_call(
        functools.partial(_xattn_kernel, scale=HEAD_DIM ** -0.5, heads=xw // HEAD_DIM, groups=groups),
        grid=(n // tm,),
        in_specs=[pl.BlockSpec((tm, xw), lambda i: (i, 0)),
                  pl.BlockSpec((groups * n_mem, xw), mem), pl.BlockSpec((groups * n_mem, xw), mem)],
        out_specs=pl.BlockSpec((tm, xw), lambda i: (i, 0)),
        out_shape=jax.ShapeDtypeStruct((n, xw), F32),
        compiler_params=_cparams("arbitrary"),
        name="xattn",
    )(q, mk, mv)


def _store_slabs(ref, x):
    m = x.shape[0]
    for s in range(SLAB_ROWS):
        ref[pl.ds(s, m, stride=SLAB_ROWS), :] = x[:, s * LANES:(s + 1) * LANES]


def _load_slabs(ref, m, first_row=0):
    return jnp.concatenate(
        [ref[pl.ds(first_row + s, m, stride=SLAB_ROWS), :] for s in range(SLAB_ROWS)], axis=1)


def _top_k_route(logits, counts, n_exp):
    tm = logits.shape[0]
    lane = lax.broadcasted_iota(jnp.int32, (tm, LANES), 1).astype(F32)
    lg = jnp.where(lane < n_exp, logits, NEG_INF)
    vals, ids, picks = [], [], []
    for _ in range(TOP_K):
        v = jnp.max(lg, axis=-1, keepdims=True)
        e = jnp.min(jnp.where(lg == v, lane, float(LANES)), axis=-1, keepdims=True)
        pick = lane == e
        lg = jnp.where(pick, NEG_INF, lg)
        vals.append(v), ids.append(e), picks.append(pick)
    exps = [jnp.exp(v - vals[0]) for v in vals]
    total = functools.reduce(lambda a, b: a + b, exps)
    chosen = functools.reduce(lambda a, b: a + b, [jnp.where(p, 1.0, 0.0) for p in picks])
    r = lax.broadcasted_iota(jnp.int32, (tm, tm), 0)
    c = lax.broadcasted_iota(jnp.int32, (tm, tm), 1)
    earlier = jnp.where(c < r, 1.0, 0.0).astype(BF16)
    before = _dot(earlier, chosen.astype(BF16)) + counts
    table = jnp.zeros((tm, LANES), F32)
    for k in range(TOP_K):
        rank = jnp.sum(jnp.where(picks[k], before, 0.0), axis=-1, keepdims=True)
        table = jnp.where(lane == k, exps[k] / total, table)
        table = jnp.where(lane == TOP_K + k, ids[k], table)
        table = jnp.where(lane == 2 * TOP_K + k, rank, table)
    return table, counts + jnp.sum(chosen, axis=0, keepdims=True)


def _xattn_out_kernel(x_ref, o_ref, wo_ref, g_ref, wr_ref, br_ref, cnt_ref,
                      x2_ref, xn_ref, rt_ref, cnt_out_ref, cnt_scr, *, n_exp):
    @pl.when(pl.program_id(0) == 0)
    def _():
        cnt_scr[...] = cnt_ref[...]

    x2 = x_ref[...] + _dot(o_ref[...].astype(BF16), wo_ref[...])
    x2_ref[...] = x2
    xn = _rms_f32(x2, g_ref[...])
    _store_slabs(xn_ref, xn)
    logits = _dot(xn.astype(BF16), wr_ref[...]) + br_ref[...]
    table, counts = _top_k_route(logits, cnt_scr[...], n_exp)
    rt_ref[...] = table
    cnt_scr[...] = counts
    cnt_out_ref[...] = counts


def _xattn_out(x, o, wo, g, wr, br, counts, *, n_exp, tm):
    n, d = x.shape
    assert d == SLAB_ROWS * LANES
    row = lambda i: (i, 0)
    fixed = lambda i: (0, 0)
    return pl.pallas_call(
        functools.partial(_xattn_out_kernel, n_exp=n_exp),
        grid=(n // tm,),
        in_specs=[pl.BlockSpec((tm, d), row), pl.BlockSpec((tm, o.shape[1]), row),
                  pl.BlockSpec(wo.shape, fixed), pl.BlockSpec((1, d), fixed),
                  pl.BlockSpec(wr.shape, fixed), pl.BlockSpec((1, LANES), fixed),
                  pl.BlockSpec((1, LANES), fixed)],
        out_specs=[pl.BlockSpec((tm, d), row), pl.BlockSpec((tm * SLAB_ROWS, LANES), row),
                   pl.BlockSpec((tm, LANES), row), pl.BlockSpec((1, LANES), fixed)],
        out_shape=[jax.ShapeDtypeStruct((n, d), F32), jax.ShapeDtypeStruct((n * SLAB_ROWS, LANES), F32),
                   jax.ShapeDtypeStruct((n, LANES), F32), jax.ShapeDtypeStruct((1, LANES), F32)],
        scratch_shapes=[pltpu.VMEM((1, LANES), F32)],
        compiler_params=_cparams("arbitrary"),
        name="xattn_out",
    )(x, o, wo, g, wr, br, counts)


def _slab(ref, row):
    return ref.at[pl.ds(pl.multiple_of(row * SLAB_ROWS, SLAB_ROWS), SLAB_ROWS)]


def _dispatch_kernel(dest_ref, x_ref, xs_in, xs_hbm, sem):
    del xs_in
    n_tok = dest_ref.shape[1] // TOP_K

    def slab_copies(t):
        src = _slab(x_ref, t)
        return [pltpu.make_async_copy(src, _slab(xs_hbm, dest_ref[0, t * TOP_K + k]), sem)
                for k in range(TOP_K)]

    def start(t, c):
        for cp in slab_copies(t):
            cp.start()
        return c

    def wait(t, c):
        for cp in slab_copies(t):
            cp.wait()
        return c

    lax.fori_loop(0, n_tok, start, 0, unroll=2)
    lax.fori_loop(0, n_tok, wait, 0, unroll=2)


def _dispatch(xn_slabs, dest, xs, *, tc):
    n = xn_slabs.shape[0] // SLAB_ROWS
    return pl.pallas_call(
        _dispatch_kernel,
        grid=(n // tc,),
        in_specs=[pl.BlockSpec((None, 1, tc * TOP_K), lambda i: (i, 0, 0), memory_space=pltpu.SMEM),
                  pl.BlockSpec((tc * SLAB_ROWS, LANES), lambda i: (i, 0)),
                  pl.BlockSpec(memory_space=pl.ANY)],
        out_specs=pl.BlockSpec(memory_space=pl.ANY),
        out_shape=jax.ShapeDtypeStruct(xs.shape, xs.dtype),
        scratch_shapes=[pltpu.SemaphoreType.DMA(())],
        input_output_aliases={2: 0},
        compiler_params=_cparams("arbitrary"),
        name="moe_dispatch",
    )(dest, xn_slabs, xs)


def _for_valid_rows(tile, n_used, valid_ref, tm, compute, zero_all):
    used = tile < n_used
    valid = valid_ref[tile]
    sub = tm // ROW_VARIANTS

    @pl.when(jnp.logical_not(used))
    def _():
        zero_all()

    for i in range(ROW_VARIANTS):
        lo, m = i * sub, (i + 1) * sub
        in_range = valid > lo if m == tm else jnp.logical_and(valid > lo, valid <= m)

        @pl.when(jnp.logical_and(used, in_range))
        def _(m=m):
            compute(m)


def _gate_up_kernel(be_ref, nblk_ref, valid_ref, x_ref, wg_ref, wl_ref, bg_ref, bl_ref, h_ref):
    del be_ref
    tm = h_ref.shape[0]

    def zero_all():
        h_ref[...] = jnp.zeros_like(h_ref)

    def compute(m):
        x = _load_slabs(x_ref, m).astype(BF16)
        gate = jnp.minimum(_dot(x, wg_ref[...].astype(BF16)) + bg_ref[...], SWIGLU_LIMIT)
        lin = jnp.clip(_dot(x, wl_ref[...].astype(BF16)) + bl_ref[...], -SWIGLU_LIMIT, SWIGLU_LIMIT)
        h_ref[:m, :] = ((lin + 1.0) * gate * jax.nn.sigmoid(SWIGLU_ALPHA * gate)).astype(h_ref.dtype)
        if m < tm:
            h_ref[m:, :] = jnp.zeros((tm - m, h_ref.shape[1]), h_ref.dtype)

    _for_valid_rows(pl.program_id(1), nblk_ref[0], valid_ref, tm, compute, zero_all)


def _gate_up(xs, w_gu, b_gu, blk_e, nblk, valid, *, tm, tn):
    rows = xs.shape[0] // SLAB_ROWS
    n_exp, d, de2 = w_gu.shape
    de = de2 // 2
    nj = de // tn
    used = lambda j, r, be, nb, vl: (jnp.minimum(r, nb[0] - 1), 0)
    exp_col = lambda off: (lambda j, r, be, nb, vl: (be[jnp.minimum(r, nb[0] - 1)], 0, j + off))
    grid_spec = pltpu.PrefetchScalarGridSpec(
        num_scalar_prefetch=3,
        grid=(nj, rows // tm),
        in_specs=[pl.BlockSpec((tm * SLAB_ROWS, LANES), used),
                  pl.BlockSpec((None, d, tn), exp_col(0)), pl.BlockSpec((None, d, tn), exp_col(nj)),
                  pl.BlockSpec((None, 1, tn), exp_col(0)), pl.BlockSpec((None, 1, tn), exp_col(nj))],
        out_specs=pl.BlockSpec((tm, tn), lambda j, r, be, nb, vl: (r, j)),
    )
    b3 = b_gu.reshape(n_exp, 1, de2)
    return pl.pallas_call(
        _gate_up_kernel,
        grid_spec=grid_spec,
        out_shape=jax.ShapeDtypeStruct((rows, de), BF16),
        compiler_params=_cparams("arbitrary", "arbitrary"),
        name="moe_gate_up",
    )(blk_e, nblk, valid, xs, w_gu, w_gu, b3, b3)


def _down_kernel(be_ref, nblk_ref, valid_ref, h_ref, w_ref, b_ref, o_ref):
    del be_ref
    tm = h_ref.shape[0]

    def zero_all():
        o_ref[...] = jnp.zeros_like(o_ref)

    def compute(m):
        out = _dot(h_ref[:m, :], w_ref[...].astype(BF16)) + b_ref[...]
        _store_slabs(o_ref, out)
        if m < tm:
            o_ref[m * SLAB_ROWS:, :] = jnp.zeros(((tm - m) * SLAB_ROWS, LANES), o_ref.dtype)

    _for_valid_rows(pl.program_id(0), nblk_ref[0], valid_ref, tm, compute, zero_all)


def _down(hdn, w_down, b_down, blk_e, nblk, valid, *, tm):
    rows, de = hdn.shape
    n_exp, _, d = w_down.shape
    assert d == SLAB_ROWS * LANES
    used = lambda r, be, nb, vl: (jnp.minimum(r, nb[0] - 1), 0)
    expert = lambda r, be, nb, vl: (be[jnp.minimum(r, nb[0] - 1)], 0, 0)
    grid_spec = pltpu.PrefetchScalarGridSpec(
        num_scalar_prefetch=3,
        grid=(rows // tm,),
        in_specs=[pl.BlockSpec((tm, de), used),
                  pl.BlockSpec((None, de, d), expert), pl.BlockSpec((None, 1, d), expert)],
        out_specs=pl.BlockSpec((tm * SLAB_ROWS, LANES), lambda r, be, nb, vl: (r, 0)),
    )
    return pl.pallas_call(
        _down_kernel,
        grid_spec=grid_spec,
        out_shape=jax.ShapeDtypeStruct((rows * SLAB_ROWS, LANES), F32),
        compiler_params=_cparams("arbitrary"),
        name="moe_down",
    )(blk_e, nblk, valid, hdn, w_down, b_down.reshape(n_exp, 1, d))


def _combine_kernel(pos_ref, x_ref, gate_ref, rows_hbm, g_ref, y_ref, buf, sem):
    tc = x_ref.shape[0]
    hc = tc // 2
    per_half = TOP_K * hc

    def slab_copy(half, i):
        return pltpu.make_async_copy(_slab(rows_hbm, pos_ref[0, half * per_half + i]),
                                     _slab(buf.at[half], i), sem.at[half])

    for half in range(2):
        lax.fori_loop(0, per_half, lambda i, c, half=half: (slab_copy(half, i).start(), c)[1], 0, unroll=8)
    for half in range(2):
        lax.fori_loop(0, per_half, lambda i, c, half=half: (slab_copy(half, i).wait(), c)[1], 0, unroll=8)
        rows = slice(half * hc, (half + 1) * hc)
        x3 = x_ref[rows, :]
        for k in range(TOP_K):
            x3 = x3 + gate_ref[rows, k:k + 1] * _load_slabs(buf.at[half], hc, first_row=k * hc * SLAB_ROWS)
        y_ref[rows, :] = _rms_f32(x3, g_ref[...])


def _combine(x, gates, rows, pos, g, *, tc):
    n, d = x.shape
    return pl.pallas_call(
        _combine_kernel,
        grid=(n // tc,),
        in_specs=[pl.BlockSpec((None, 1, TOP_K * tc), lambda i: (i, 0, 0), memory_space=pltpu.SMEM),
                  pl.BlockSpec((tc, d), lambda i: (i, 0)),
                  pl.BlockSpec((tc, TOP_K), lambda i: (i, 0)),
                  pl.BlockSpec(memory_space=pl.ANY),
                  pl.BlockSpec((1, d), lambda i: (0, 0))],
        out_specs=pl.BlockSpec((tc, d), lambda i: (i, 0)),
        out_shape=jax.ShapeDtypeStruct((n, d), F32),
        scratch_shapes=[pltpu.VMEM((2, TOP_K * (tc // 2) * SLAB_ROWS, LANES), F32),
                        pltpu.SemaphoreType.DMA((2,))],
        compiler_params=_cparams("arbitrary"),
        name="moe_combine",
    )(pos, x, gates, rows, g)


def _row_layout(table, counts, n_exp, tm):
    n_tok = table.shape[0]
    gates = table[:, :TOP_K]
    expert = table[:, TOP_K:2 * TOP_K].astype(jnp.int32)
    rank = table[:, 2 * TOP_K:3 * TOP_K].astype(jnp.int32)
    counts = counts.astype(jnp.int32)
    padded = (counts + tm - 1) // tm * tm
    pad_ends = jnp.cumsum(padded)
    pad_starts = pad_ends - padded
    is_e = expert[:, :, None] == jnp.arange(n_exp, dtype=jnp.int32)[None, None, :]
    dest = rank + jnp.sum(jnp.where(is_e, pad_starts[None, None, :], 0), axis=-1)
    n_blocks = -(-n_tok * TOP_K // tm) + n_exp
    tile_start = jnp.arange(n_blocks, dtype=jnp.int32) * tm
    blk_e = jnp.minimum(jnp.sum((pad_ends[None, :] <= tile_start[:, None]).astype(jnp.int32), axis=1),
                        n_exp - 1)
    nblk = (pad_ends[-1] // tm).reshape(1)
    is_blk_e = blk_e[:, None] == jnp.arange(n_exp, dtype=jnp.int32)[None, :]
    real_end = jnp.sum(jnp.where(is_blk_e, (pad_starts + counts)[None, :], 0), axis=1)
    valid = jnp.clip(real_end - tile_start, 0, tm).astype(jnp.int32)
    return gates, dest, blk_e, nblk, valid, n_blocks


def _moe_final(x2_groups, xn_groups, table_groups, counts, w_gu, b_gu, w_down, b_down, g_final):
    n_exp = w_gu.shape[0]
    tm = min(EXPERT_TILE, TOP_K * sum(x.shape[0] for x in x2_groups))
    gates, dest, blk_e, nblk, valid, n_blocks = _row_layout(
        jnp.concatenate(table_groups, axis=0), counts[0, :n_exp], n_exp, tm)
    xs = jnp.zeros((n_blocks * tm * SLAB_ROWS, LANES), F32)
    start = 0
    for xn in xn_groups:
        n = xn.shape[0] // SLAB_ROWS
        tc = min(COMBINE_TILE, n)
        xs = _dispatch(xn, dest[start:start + n].reshape(n // tc, 1, tc * TOP_K), xs, tc=tc)
        start += n
    hdn = _gate_up(xs, w_gu, b_gu, blk_e, nblk, valid, tm=tm, tn=min(1024, w_gu.shape[2] // 2))
    rows = _down(hdn, w_down, b_down, blk_e, nblk, valid, tm=tm)
    outs, start = [], 0
    for x2 in x2_groups:
        n = x2.shape[0]
        tc = min(COMBINE_TILE, n)
        pos = dest[start:start + n].reshape(n // tc, 2, tc // 2, TOP_K).transpose(0, 1, 3, 2)
        outs.append(_combine(x2, gates[start:start + n], rows, pos.reshape(n // tc, 1, TOP_K * tc),
                             g_final, tc=tc))
        start += n
    return outs


def _layer_common(x, attn, pool, wa, wp, g_x, wq, mk, mv, rows_per_mem, xattn_tile,
                  wo, g_f, wr, br, counts, n_exp):
    n = x.shape[0]
    x1, qx = _mix_out(x, attn, pool, wa, wp, g_x, wq, tm=min(256, n))
    o = _xattn(qx, mk, mv, rows_per_mem=rows_per_mem, n_mem=mk.shape[0] * rows_per_mem // n,
               tm=xattn_tile)
    return _xattn_out(x1, o, wo, g_f, wr, br, counts, n_exp=n_exp, tm=min(256, n))


def kernel(x_prompt, x_sample, mem_prompt, cache_k, cache_v, cache_logf, cache_mem_k, cache_mem_v,
           state_pool, page_table, g_mix, w_in, b_forget, w_pool, s_pool, w_out, g_xattn, g_mem,
           w_xq, w_xk, w_xv, w_xo, g_ffn, w_router, b_router, w_gu, b_gu, w_down, b_down, g_final):
    depth = w_in.shape[0]
    bp, tp, d = x_prompt.shape
    bs, ts, _ = x_sample.shape
    n_pool, page, heads, hd = cache_k.shape[1:]
    fw = heads * hd
    pw = state_pool.shape[-1]
    n_mem = mem_prompt.shape[1]
    xw = w_xq.shape[-1]
    n_exp = w_router.shape[-1]
    past_len = page_table.shape[1] * page
    assert hd == HEAD_DIM and page == PAGE_SIZE and heads == 8 and pw == fw
    assert w_in.shape[-1] == 3 * fw + heads + pw and state_pool.shape[2] == POOL_STATE
    assert depth == 1, "the experts of all groups are evaluated together after the last layer"

    xp = x_prompt.reshape(bp * tp, d)
    xs = x_sample.reshape(bs * ts, d)
    row = lambda a: a.reshape(1, -1)
    outs = {k: [] for k in ("kp", "vp", "lfp", "pp", "mkp", "mvp", "ks", "vs", "lfs", "ps")}
    for l in range(depth):
        w = w_in[l]
        w4 = jnp.concatenate([w[:, :3 * fw], w[:, 3 * fw + heads:]], axis=1).astype(BF16)
        wf = jnp.pad(w[:, 3 * fw:3 * fw + heads], ((0, 0), (0, LANES - heads))).astype(BF16)
        bfp = jnp.pad(b_forget[l], (0, LANES - heads)).reshape(1, LANES)
        wpool = w_pool[l].astype(BF16)
        spool = row(s_pool[l])
        wa = w_out[l, :fw].astype(BF16)
        wp = w_out[l, fw:].astype(BF16)
        wq = w_xq[l].astype(BF16)
        wkv = jnp.concatenate([w_xk[l], w_xv[l]], axis=1).astype(BF16)
        wo = w_xo[l].astype(BF16)
        wr = jnp.pad(w_router[l], ((0, 0), (0, LANES - n_exp))).astype(BF16)
        br = jnp.pad(b_router[l], (0, LANES - n_exp)).reshape(1, LANES)

        tm = min(512, tp)
        q, k, v, u, kb, vb, lf, ct = _in_proj(xp, row(g_mix[l]), w4, wf, bfp, seq_len=tp, heads=heads, tm=tm)
        attn = _fox_prompt(q, kb, vb, ct, batch=bp, seq_len=tp, heads=heads, tq=tm,
                           heads_per_step=FLASH_HEADS_PER_STEP)
        pool = _pool_prompt(u, wpool, spool, seq_len=tp, tm=tm)
        outs["kp"].append(k.reshape(bp, tp, heads, hd))
        outs["vp"].append(v.reshape(bp, tp, heads, hd))
        outs["lfp"].append(lf.reshape(bp, tp, heads))
        outs["pp"].append(u.reshape(bp, tp, pw)[:, tp - POOL_STATE:])
        mk, mv = _mem_kv(mem_prompt.reshape(bp * n_mem, d), row(g_mem[l]), wkv, tm=min(256, bp * n_mem))
        outs["mkp"].append(mk.reshape(bp, n_mem, xw // HEAD_DIM, HEAD_DIM))
        outs["mvp"].append(mv.reshape(bp, n_mem, xw // HEAD_DIM, HEAD_DIM))
        x2p, xnp_, rtp, counts = _layer_common(xp, attn, pool, wa, wp, row(g_xattn[l]), wq, mk, mv, tp, tm,
                                               wo, row(g_ffn[l]), wr, br, jnp.zeros((1, LANES), F32), n_exp)

        ns = bs * ts
        tms = min(512, ns)
        q, k, v, u, _, _, lf, ct = _in_proj(xs, row(g_mix[l]), w4, wf, bfp, seq_len=ts, heads=heads, tm=tms)
        c_new = ct.reshape(ns // tms, heads, tms // ts, ts).transpose(0, 2, 3, 1).reshape(bs, 1, ts * heads)
        c_new = jnp.pad(c_new, ((0, 0), (0, 0), (0, LANES - ts * heads)))
        scan = _logf_scan(cache_logf[l].reshape(n_pool, page * heads), heads=heads,
                          tm=_divisor_tile(n_pool, 512))
        attn = _fox_decode(q, k.reshape(bs, ts * heads, hd), v.reshape(bs, ts * heads, hd), c_new,
                           cache_k[l].reshape(n_pool, page * heads, hd),
                           cache_v[l].reshape(n_pool, page * heads, hd),
                           scan, page_table, heads=heads, n_new=ts,
                           pages_per_step=_divisor_tile(page_table.shape[1], DECODE_PAGES_PER_STEP, 1))
        u3 = u.reshape(bs, ts, pw)
        pool = _pool_sample(u3.transpose(1, 0, 2), state_pool[l].transpose(1, 0, 2), wpool, spool,
                            past_len=past_len)
        pool = pool.transpose(1, 0, 2).reshape(ns, pw)
        outs["ks"].append(k.reshape(bs, ts, heads, hd))
        outs["vs"].append(v.reshape(bs, ts, heads, hd))
        outs["lfs"].append(lf.reshape(bs, ts, heads))
        outs["ps"].append(jnp.concatenate([state_pool[l], u3], axis=1)[:, ts:])
        mks = cache_mem_k[l].reshape(bs * n_mem, xw)
        mvs = cache_mem_v[l].reshape(bs * n_mem, xw)
        x2s, xns, rts, counts = _layer_common(xs, attn, pool, wa, wp, row(g_xattn[l]), wq, mks, mvs, ts,
                                              ts * _divisor_tile(bs, XATTN_SEQS_PER_STEP, 1),
                                              wo, row(g_ffn[l]), wr, br, counts, n_exp)

        yp, ys = _moe_final([x2p, x2s], [xnp_, xns], [rtp, rts], counts, w_gu[l], b_gu[l], w_down[l],
                            b_down[l], row(g_final))

    st = lambda name: jnp.stack(outs[name])
    return (yp.reshape(bp, tp, d), ys.reshape(bs, ts, d),
            st("kp"), st("vp"), st("lfp"), st("pp"), st("mkp"), st("mvp"),
            st("ks"), st("vs"), st("lfs"), st("ps"))
```

```python
import functools

import jax
import jax.numpy as jnp
from jax import lax
from jax.experimental import pallas as pl
from jax.experimental.pallas import tpu as pltpu

F32 = jnp.float32
BF16 = jnp.bfloat16

HEAD_DIM = 128
PAGE_SIZE = 128
POOL_WINDOWS = (2, 4, 8, 16)
POOL_STATE = max(POOL_WINDOWS) - 1
POOL_HALO = 16
TOP_K = 4
SWIGLU_ALPHA = 1.702
SWIGLU_LIMIT = 7.0
NORM_EPS = 1e-5
LANES = 128
BF16_ROWS = 16
SLAB_ROWS = 16
SCAN_ROWS = 8
VMEM_LIMIT = 56 * 1024 * 1024
NEG_INF = float("-inf")
EXPERT_TILE = 512
ROW_VARIANTS = 4
WEIGHT_SPLITS = 4
COMBINE_TILE = 128


DECODE_PAGES_PER_STEP = 8
FLASH_HEADS_PER_STEP = 4
XATTN_SEQS_PER_STEP = 8


def _divisor_tile(n, cap, multiple=8):
    for t in range(min(cap, n), 0, -1):
        if n % t == 0 and t % multiple == 0:
            return t
    raise ValueError(f"no tile for {n}")


def _cparams(*sem):
    return pltpu.CompilerParams(dimension_semantics=sem, vmem_limit_bytes=VMEM_LIMIT)


def _rms_f32(x, g):
    return x * lax.rsqrt(jnp.mean(x * x, axis=-1, keepdims=True) + NORM_EPS) * g


def _dot(a, b):
    return jnp.dot(a, b, preferred_element_type=F32)


def _dot_nt(a, b):
    return lax.dot_general(a, b, (((1,), (1,)), ((), ())), preferred_element_type=F32)


def _pad_rows(x):
    rows = x.shape[0]
    if rows >= BF16_ROWS:
        return x
    return jnp.concatenate([x, jnp.zeros((BF16_ROWS - rows,) + x.shape[1:], x.dtype)], axis=0)


def _split3(x):
    hi = x.astype(BF16)
    r = x - hi.astype(F32)
    mid = r.astype(BF16)
    lo = (r - mid.astype(F32)).astype(BF16)
    return hi, mid, lo


def _lane_cumsum(x, seg):
    n = x.shape[-1]
    r = lax.broadcasted_iota(jnp.int32, (n, n), 0)
    c = lax.broadcasted_iota(jnp.int32, (n, n), 1)
    keep = r <= c
    if seg < n:
        keep = jnp.logical_and(keep, (r // seg) == (c // seg))
    tri = jnp.where(keep, 1.0, 0.0).astype(BF16)
    rows = x.shape[0]
    hi, mid, lo = _split3(_pad_rows(x))
    return (_dot(hi, tri) + _dot(mid, tri) + _dot(lo, tri))[:rows]


def _log_sigmoid(z):
    return jnp.minimum(z, 0.0) - jnp.log1p(jnp.exp(-jnp.abs(z)))


def _in_proj_kernel(x_ref, g_ref, w_ref, wf_ref, bf_ref,
                    q_ref, k_ref, v_ref, u_ref, kb_ref, vb_ref, lf_ref, ct_ref,
                    xn_scr, carry_scr, *, tiles_per_seq, seg, heads):
    i = pl.program_id(0)
    j = pl.program_id(1)

    @pl.when(j == 0)
    def _():
        xn = _rms_f32(x_ref[...], g_ref[...]).astype(BF16)
        xn_scr[...] = xn
        lf = _log_sigmoid(_dot(xn, wf_ref[...]) + bf_ref[...])
        lf_ref[...] = lf[:, :heads]
        lft = lf.T[:8]
        c = _lane_cumsum(lft, seg)

        @pl.when(i % tiles_per_seq == 0)
        def _():
            carry_scr[...] = jnp.zeros_like(carry_scr)

        c = c + carry_scr[:, :1]
        ct_ref[0] = c
        carry_scr[...] = jnp.broadcast_to(c[:, -1:], carry_scr.shape)

    y = _dot(xn_scr[...], w_ref[...])

    @pl.when(j == 0)
    def _():
        q_ref[...] = y

    @pl.when(j == 1)
    def _():
        k_ref[...] = y
        kb_ref[...] = y.astype(BF16)

    @pl.when(j == 2)
    def _():
        v_ref[...] = y
        vb_ref[...] = y.astype(BF16)

    @pl.when(j == 3)
    def _():
        u_ref[...] = y


def _in_proj(x, g, w4, wf, bfp, *, seq_len, heads, tm):
    n, d = x.shape
    width = w4.shape[1] // 4
    if seq_len >= tm:
        tiles_per_seq, seg = seq_len // tm, tm
    else:
        tiles_per_seq, seg = 1, seq_len
    n_tiles = n // tm
    row = lambda i, j: (i, 0)
    big = jax.ShapeDtypeStruct((n, width), F32)
    bigb = jax.ShapeDtypeStruct((n, width), BF16)
    return pl.pallas_call(
        functools.partial(_in_proj_kernel, tiles_per_seq=tiles_per_seq, seg=seg, heads=heads),
        grid=(n_tiles, 4),
        in_specs=[
            pl.BlockSpec((tm, d), row),
            pl.BlockSpec((1, d), lambda i, j: (0, 0)),
            pl.BlockSpec((d, width), lambda i, j: (0, j)),
            pl.BlockSpec((d, LANES), lambda i, j: (0, 0)),
            pl.BlockSpec((1, LANES), lambda i, j: (0, 0)),
        ],
        out_specs=[
            pl.BlockSpec((tm, width), row), pl.BlockSpec((tm, width), row),
            pl.BlockSpec((tm, width), row), pl.BlockSpec((tm, width), row),
            pl.BlockSpec((tm, width), row), pl.BlockSpec((tm, width), row),
            pl.BlockSpec((tm, heads), row),
            pl.BlockSpec((1, 8, tm), lambda i, j: (i, 0, 0)),
        ],
        out_shape=[big, big, big, big, bigb, bigb,
                   jax.ShapeDtypeStruct((n, heads), F32),
                   jax.ShapeDtypeStruct((n_tiles, 8, tm), F32)],
        scratch_shapes=[pltpu.VMEM((tm, d), BF16), pltpu.VMEM((8, LANES), F32)],
        compiler_params=_cparams("arbitrary", "arbitrary"),
        name="in_proj",
    )(x, g, w4, wf, bfp)


def _fox_flash_kernel(qi_ref, ki_ref, q_ref, k_ref, v_ref, ct_ref, o_ref, m_scr, l_scr, acc_scr,
                      *, scale, heads_per_step):
    hp = pl.program_id(1)
    t = pl.program_id(2)
    qi = qi_ref[t]
    ki = ki_ref[t]

    @pl.when(ki == 0)
    def _():
        m_scr[...] = jnp.full_like(m_scr, NEG_INF)
        l_scr[...] = jnp.zeros_like(l_scr)
        acc_scr[...] = jnp.zeros_like(acc_scr)

    def step(diagonal):
        for hh in range(heads_per_step):
            sl = slice(hh * HEAD_DIM, (hh + 1) * HEAD_DIM)
            q = (q_ref[:, sl] * scale).astype(BF16)
            s = _dot_nt(q, k_ref[:, sl]) - ct_ref[0, pl.ds(hp * heads_per_step + hh, 1), :]
            if diagonal:
                r = lax.broadcasted_iota(jnp.int32, s.shape, 0)
                c = lax.broadcasted_iota(jnp.int32, s.shape, 1)
                s = jnp.where(c <= r, s, NEG_INF)
            m_prev = m_scr[hh]
            m_new = jnp.maximum(m_prev, jnp.max(s, axis=-1, keepdims=True))
            alpha = jnp.exp(m_prev - m_new)
            p = jnp.exp(s - m_new[:, :1])
            l_new = alpha * l_scr[hh] + jnp.sum(p, axis=-1, keepdims=True)
            acc = alpha * acc_scr[hh] + _dot(p.astype(BF16), v_ref[:, sl])
            if diagonal:
                o_ref[:, sl] = (acc / l_new).astype(o_ref.dtype)
            else:
                l_scr[hh] = l_new
                acc_scr[hh] = acc
                m_scr[hh] = m_new

    @pl.when(ki < qi)
    def _():
        step(False)

    @pl.when(ki == qi)
    def _():
        step(True)


def _fox_prompt(q, kb, vb, ct, *, batch, seq_len, heads, tq, heads_per_step):
    n, width = q.shape
    nq = seq_len // tq
    hw = heads_per_step * HEAD_DIM
    pairs = [(qi, ki) for qi in range(nq) for ki in range(qi + 1)]
    qi_tab = jnp.array([p[0] for p in pairs], jnp.int32)
    ki_tab = jnp.array([p[1] for p in pairs], jnp.int32)
    q_blk = lambda b, hp, t, qt, kt: (b * nq + qt[t], hp)
    kv_blk = lambda b, hp, t, qt, kt: (b * nq + kt[t], hp)
    grid_spec = pltpu.PrefetchScalarGridSpec(
        num_scalar_prefetch=2,
        grid=(batch, heads // heads_per_step, len(pairs)),
        in_specs=[
            pl.BlockSpec((tq, hw), q_blk),
            pl.BlockSpec((tq, hw), kv_blk),
            pl.BlockSpec((tq, hw), kv_blk),
            pl.BlockSpec((1, 8, tq), lambda b, hp, t, qt, kt: (b * nq + kt[t], 0, 0)),
        ],
        out_specs=pl.BlockSpec((tq, hw), q_blk),
        scratch_shapes=[pltpu.VMEM((heads_per_step, tq, LANES), F32),
                        pltpu.VMEM((heads_per_step, tq, LANES), F32),
                        pltpu.VMEM((heads_per_step, tq, HEAD_DIM), F32)],
    )
    return pl.pallas_call(
        functools.partial(_fox_flash_kernel, scale=HEAD_DIM ** -0.5, heads_per_step=heads_per_step),
        grid_spec=grid_spec,
        out_shape=jax.ShapeDtypeStruct((n, width), BF16),
        compiler_params=_cparams("arbitrary", "arbitrary", "arbitrary"),
        name="fox_prompt",
    )(qi_tab, ki_tab, q, kb, vb, ct)


def _logf_scan_kernel(x_ref, o_ref, *, heads):
    n = x_ref.shape[1]
    r = lax.broadcasted_iota(jnp.int32, (n, n), 0)
    c = lax.broadcasted_iota(jnp.int32, (n, n), 1)
    same_head = jnp.bitwise_and(r, heads - 1) == jnp.bitwise_and(c, heads - 1)
    m_tot = jnp.where(same_head, 1.0, 0.0).astype(BF16)
    m_cum = jnp.where(jnp.logical_and(same_head, r <= c), 1.0, 0.0).astype(BF16)
    hi, mid, lo = _split3(x_ref[...])
    o_ref[:, :n] = _dot(hi, m_cum) + _dot(mid, m_cum) + _dot(lo, m_cum)
    o_ref[:, n:] = _dot(hi, m_tot) + _dot(mid, m_tot) + _dot(lo, m_tot)


def _logf_scan(logf_pages, *, heads, tm):
    n_pool, n = logf_pages.shape
    return pl.pallas_call(
        functools.partial(_logf_scan_kernel, heads=heads),
        grid=(n_pool // tm,),
        in_specs=[pl.BlockSpec((tm, n), lambda i: (i, 0))],
        out_specs=pl.BlockSpec((tm, 2 * n), lambda i: (i, 0)),
        out_shape=jax.ShapeDtypeStruct((n_pool, 2 * n), F32),
        compiler_params=_cparams("arbitrary"),
        name="logf_scan",
    )(logf_pages)


def _fox_decode_kernel(pt_ref, q_ref, *refs, scale, heads, n_new, pages_per_step):
    pps = pages_per_step
    k_refs, v_refs, scan_refs = refs[:pps], refs[pps:2 * pps], refs[2 * pps:3 * pps]
    kn_ref, vn_ref, cn_ref, o_ref, m_scr, l_scr, acc_scr, carry_scr = refs[3 * pps:]
    step = pl.program_id(1)
    last = pl.num_programs(1) - 1
    rows = heads * n_new
    page_rows = PAGE_SIZE * heads

    @pl.when(step == 0)
    def _():
        m_scr[...] = jnp.full_like(m_scr, NEG_INF)
        l_scr[...] = jnp.zeros_like(l_scr)
        acc_scr[...] = jnp.zeros_like(acc_scr)
        carry_scr[...] = jnp.zeros_like(carry_scr)

    qb = jnp.concatenate([q_ref[:, h * HEAD_DIM:(h + 1) * HEAD_DIM] for h in range(heads)], axis=0)
    qb = (qb * scale).astype(BF16)

    def update(s, pv):
        m_prev = m_scr[...]
        m_new = jnp.maximum(m_prev, jnp.max(s, axis=-1, keepdims=True))
        alpha = jnp.exp(m_prev - m_new)
        p = jnp.exp(s - m_new[:, :1])
        l_scr[...] = alpha * l_scr[...] + jnp.sum(p, axis=-1, keepdims=True)
        acc_scr[...] = alpha * acc_scr[...] + pv(p.astype(BF16))
        m_scr[...] = m_new

    r = lax.broadcasted_iota(jnp.int32, (rows, page_rows), 0)
    c = lax.broadcasted_iota(jnp.int32, (rows, page_rows), 1)
    head_ok = jnp.bitwise_and(c, heads - 1) == r // n_new
    carry = carry_scr[...]
    parts = []
    for i in range(pps):
        s = _dot_nt(qb, k_refs[i][0].astype(BF16))
        page = pt_ref[(pl.program_id(0) * pl.num_programs(1) + step) * pps + i]
        scan = scan_refs[i][pl.ds(page % SCAN_ROWS, 1), :]
        parts.append(jnp.where(head_ok, s - (scan[:, :page_rows] + carry), NEG_INF))
        carry = carry + scan[:, page_rows:]
    carry_scr[...] = carry

    def page_values(p):
        out = _dot(p[:, :page_rows], v_refs[0][0].astype(BF16))
        for i in range(1, pps):
            out = out + _dot(p[:, i * page_rows:(i + 1) * page_rows], v_refs[i][0].astype(BF16))
        return out

    update(jnp.concatenate(parts, axis=1), page_values)

    @pl.when(step == last)
    def _():
        pad = jnp.zeros((LANES - rows, HEAD_DIM), F32)
        rn = lax.broadcasted_iota(jnp.int32, (rows, LANES), 0)
        cn = lax.broadcasted_iota(jnp.int32, (rows, LANES), 1)
        ok = jnp.logical_and(jnp.bitwise_and(cn, heads - 1) == rn // n_new,
                             cn // heads <= jnp.bitwise_and(rn, n_new - 1))
        s = _dot_nt(qb, jnp.concatenate([kn_ref[0], pad], axis=0).astype(BF16))
        s = jnp.where(ok, s - (cn_ref[0] + carry[:, :LANES]), NEG_INF)
        vn = jnp.concatenate([vn_ref[0], pad], axis=0).astype(BF16)
        update(s, lambda p: _dot(p, vn))
        out = acc_scr[...] / l_scr[...]
        for h in range(heads):
            o_ref[:, h * HEAD_DIM:(h + 1) * HEAD_DIM] = out[h * n_new:(h + 1) * n_new]


def _fox_decode(q, k_new, v_new, c_new, kc, vc, scan, page_table, *, heads, n_new, pages_per_step):
    n, width = q.shape
    n_seq, n_pages = page_table.shape
    pps = pages_per_step
    page_rows = PAGE_SIZE * heads
    seq = lambda b, p, pt: (b, 0, 0)
    page = lambda i: (lambda b, p, pt: (pt[b * n_pages + p * pps + i], 0, 0))
    kv_specs = [pl.BlockSpec((1, page_rows, HEAD_DIM), page(i)) for i in range(pps)]
    scan_page = lambda i: (lambda b, p, pt: (pt[b * n_pages + p * pps + i] // SCAN_ROWS, 0))
    scan_specs = [pl.BlockSpec((SCAN_ROWS, 2 * page_rows), scan_page(i)) for i in range(pps)]
    grid_spec = pltpu.PrefetchScalarGridSpec(
        num_scalar_prefetch=1,
        grid=(n_seq, n_pages // pps),
        in_specs=[pl.BlockSpec((n_new, width), lambda b, p, pt: (b, 0))]
        + kv_specs + kv_specs + scan_specs
        + [pl.BlockSpec((1, n_new * heads, HEAD_DIM), seq), pl.BlockSpec((1, n_new * heads, HEAD_DIM), seq),
           pl.BlockSpec((1, 1, LANES), seq)],
        out_specs=pl.BlockSpec((n_new, width), lambda b, p, pt: (b, 0)),
        scratch_shapes=[pltpu.VMEM((heads * n_new, LANES), F32), pltpu.VMEM((heads * n_new, LANES), F32),
                        pltpu.VMEM((heads * n_new, HEAD_DIM), F32), pltpu.VMEM((1, page_rows), F32)],
    )
    return pl.pallas_call(
        functools.partial(_fox_decode_kernel, scale=HEAD_DIM ** -0.5, heads=heads, n_new=n_new,
                          pages_per_step=pps),
        grid_spec=grid_spec,
        out_shape=jax.ShapeDtypeStruct((n, width), F32),
        compiler_params=_cparams("arbitrary", "arbitrary"),
        name="fox_decode",
    )(page_table.reshape(-1), q, *([kc] * pps), *([vc] * pps), *([scan] * pps), k_new, v_new, c_new)


def _pool_prompt_kernel(u_ref, halo_ref, w_ref, s_ref, o_ref, *, tiles_per_seq, gd):
    tm = u_ref.shape[0]
    tile_in_seq = pl.program_id(0) % tiles_per_seq
    pos = tile_in_seq * tm + lax.broadcasted_iota(jnp.int32, (tm, 1), 0)
    for g, w in enumerate(POOL_WINDOWS):
        sl = slice(g * gd, (g + 1) * gd)
        u = u_ref[:, sl]
        halo = jnp.where(tile_in_seq > 0, halo_ref[:, sl], 0.0)
        acc = jnp.concatenate([halo, u], axis=0)
        k = 1
        while k < w:
            acc = acc + pltpu.roll(acc, k, axis=0)
            k *= 2
        cnt = jnp.minimum(pos + 1, w).astype(F32)
        d = acc[POOL_HALO:] / cnt - u
        o_ref[:, sl] = (_dot(d.astype(BF16), w_ref[g]) * s_ref[:, sl]).astype(o_ref.dtype)


def _pool_prompt(u, w_pool, s_pool, *, seq_len, tm):
    n, pw = u.shape
    gd = pw // len(POOL_WINDOWS)
    halo_blocks = tm // POOL_HALO
    return pl.pallas_call(
        functools.partial(_pool_prompt_kernel, tiles_per_seq=seq_len // tm, gd=gd),
        grid=(n // tm,),
        in_specs=[
            pl.BlockSpec((tm, pw), lambda i: (i, 0)),
            pl.BlockSpec((POOL_HALO, pw), lambda i: (jnp.maximum(i * halo_blocks - 1, 0), 0)),
            pl.BlockSpec(w_pool.shape, lambda i: (0, 0, 0)),
            pl.BlockSpec((1, pw), lambda i: (0, 0)),
        ],
        out_specs=pl.BlockSpec((tm, pw), lambda i: (i, 0)),
        out_shape=jax.ShapeDtypeStruct((n, pw), BF16),
        compiler_params=_cparams("arbitrary"),
        name="pool_prompt",
    )(u, u, w_pool, s_pool)


def _pool_sample_kernel(u_ref, st_ref, w_ref, s_ref, o_ref, *, gd, past_len):
    n_new = u_ref.shape[0]

    def ext(j, sl):
        return st_ref[j, :, sl] if j < POOL_STATE else u_ref[j - POOL_STATE, :, sl]

    for g, w in enumerate(POOL_WINDOWS):
        sl = slice(g * gd, (g + 1) * gd)
        for t in range(n_new):
            cur = ext(POOL_STATE + t, sl)
            acc = cur
            for j in range(1, w):
                acc = acc + ext(POOL_STATE + t - j, sl)
            d = acc / float(min(past_len + t + 1, w)) - cur
            o_ref[t, :, sl] = (_dot(d.astype(BF16), w_ref[g]) * s_ref[:, sl]).astype(o_ref.dtype)


def _pool_sample(u_t, st_t, w_pool, s_pool, *, past_len):
    n_new, n_seq, pw = u_t.shape
    gd = pw // len(POOL_WINDOWS)
    return pl.pallas_call(
        functools.partial(_pool_sample_kernel, gd=gd, past_len=past_len),
        out_shape=jax.ShapeDtypeStruct((n_new, n_seq, pw), BF16),
        compiler_params=pltpu.CompilerParams(vmem_limit_bytes=VMEM_LIMIT),
        name="pool_sample",
    )(u_t, st_t, w_pool, s_pool)


def _mix_out_kernel(x_ref, a_ref, p_ref, wa_ref, wp_ref, g_ref, wq_ref, x1_ref, qx_ref):
    x1 = x_ref[...] + _dot(a_ref[...].astype(BF16), wa_ref[...]) + _dot(p_ref[...], wp_ref[...])
    x1_ref[...] = x1
    qx_ref[...] = _dot(_rms_f32(x1, g_ref[...]).astype(BF16), wq_ref[...])


def _mix_out(x, attn, pool, wa, wp, g, wq, *, tm):
    n, d = x.shape
    xw = wq.shape[1]
    row = lambda i: (i, 0)
    fixed = lambda i: (0, 0)
    return pl.pallas_call(
        _mix_out_kernel,
        grid=(n // tm,),
        in_specs=[
            pl.BlockSpec((tm, d), row), pl.BlockSpec((tm, attn.shape[1]), row),
            pl.BlockSpec((tm, pool.shape[1]), row),
            pl.BlockSpec(wa.shape, fixed), pl.BlockSpec(wp.shape, fixed),
            pl.BlockSpec((1, d), fixed), pl.BlockSpec(wq.shape, fixed),
        ],
        out_specs=[pl.BlockSpec((tm, d), row), pl.BlockSpec((tm, xw), row)],
        out_shape=[jax.ShapeDtypeStruct((n, d), F32), jax.ShapeDtypeStruct((n, xw), F32)],
        compiler_params=_cparams("arbitrary"),
        name="mix_out",
    )(x, attn, pool, wa, wp, g, wq)


def _mem_kv_kernel(m_ref, g_ref, w_ref, k_ref, v_ref):
    y = _dot(_rms_f32(m_ref[...], g_ref[...]).astype(BF16), w_ref[...])
    xw = k_ref.shape[1]
    k_ref[...] = y[:, :xw]
    v_ref[...] = y[:, xw:]


def _mem_kv(mem, g, wkv, *, tm):
    n, d = mem.shape
    xw = wkv.shape[1] // 2
    out = jax.ShapeDtypeStruct((n, xw), F32)
    return pl.pallas_call(
        _mem_kv_kernel,
        grid=(n // tm,),
        in_specs=[pl.BlockSpec((tm, d), lambda i: (i, 0)), pl.BlockSpec((1, d), lambda i: (0, 0)),
                  pl.BlockSpec(wkv.shape, lambda i: (0, 0))],
        out_specs=[pl.BlockSpec((tm, xw), lambda i: (i, 0))] * 2,
        out_shape=[out, out],
        compiler_params=_cparams("arbitrary"),
        name="mem_kv",
    )(mem, g, wkv)


def _xattn_kernel(q_ref, k_ref, v_ref, o_ref, *, scale, heads, groups):
    tq = q_ref.shape[0] // groups
    n_mem = k_ref.shape[0] // groups
    for g in range(groups):
        rows = slice(g * tq, (g + 1) * tq)
        mem = slice(g * n_mem, (g + 1) * n_mem)
        for h in range(heads):
            sl = slice(h * HEAD_DIM, (h + 1) * HEAD_DIM)
            q = _pad_rows(q_ref[rows, sl] * scale).astype(BF16)
            s = _dot_nt(q, k_ref[mem, sl].astype(BF16))
            p = jnp.exp(s - jnp.max(s, axis=-1, keepdims=True))
            o = _dot(p.astype(BF16), v_ref[mem, sl].astype(BF16)) / jnp.sum(p, axis=-1, keepdims=True)
            o_ref[rows, sl] = o[:tq]


def _xattn(q, mk, mv, *, rows_per_mem, n_mem, tm):
    n, xw = q.shape
    if tm <= rows_per_mem:
        groups, tiles_per_mem = 1, rows_per_mem // tm
        mem = lambda i: (i // tiles_per_mem, 0)
    else:
        groups = tm // rows_per_mem
        mem = lambda i: (i, 0)
    return pl.pallas_call(
        functools.partial(_xattn_kernel, scale=HEAD_DIM ** -0.5, heads=xw // HEAD_DIM, groups=groups),
        grid=(n // tm,),
        in_specs=[pl.BlockSpec((tm, xw), lambda i: (i, 0)),
                  pl.BlockSpec((groups * n_mem, xw), mem), pl.BlockSpec((groups * n_mem, xw), mem)],
        out_specs=pl.BlockSpec((tm, xw), lambda i: (i, 0)),
        out_shape=jax.ShapeDtypeStruct((n, xw), F32),
        compiler_params=_cparams("arbitrary"),
        name="xattn",
    )(q, mk, mv)


def _store_slabs(ref, x):
    m = x.shape[0]
    for s in range(SLAB_ROWS):
        ref[pl.ds(s, m, stride=SLAB_ROWS), :] = x[:, s * LANES:(s + 1) * LANES]


def _load_slabs(ref, m, first_row=0):
    return jnp.concatenate(
        [ref[pl.ds(first_row + s, m, stride=SLAB_ROWS), :] for s in range(SLAB_ROWS)], axis=1)


def _top_k_route(logits, counts, n_exp):
    tm = logits.shape[0]
    lane = lax.broadcasted_iota(jnp.int32, (tm, LANES), 1).astype(F32)
    lg = jnp.where(lane < n_exp, logits, NEG_INF)
    vals, ids, picks = [], [], []
    for _ in range(TOP_K):
        v = jnp.max(lg, axis=-1, keepdims=True)
        e = jnp.min(jnp.where(lg == v, lane, float(LANES)), axis=-1, keepdims=True)
        pick = lane == e
        lg = jnp.where(pick, NEG_INF, lg)
        vals.append(v), ids.append(e), picks.append(pick)
    exps = [jnp.exp(v - vals[0]) for v in vals]
    total = functools.reduce(lambda a, b: a + b, exps)
    chosen = functools.reduce(lambda a, b: a + b, [jnp.where(p, 1.0, 0.0) for p in picks])
    r = lax.broadcasted_iota(jnp.int32, (tm, tm), 0)
    c = lax.broadcasted_iota(jnp.int32, (tm, tm), 1)
    earlier = jnp.where(c < r, 1.0, 0.0).astype(BF16)
    before = _dot(earlier, chosen.astype(BF16)) + counts
    table = jnp.zeros((tm, LANES), F32)
    for k in range(TOP_K):
        rank = jnp.sum(jnp.where(picks[k], before, 0.0), axis=-1, keepdims=True)
        table = jnp.where(lane == k, exps[k] / total, table)
        table = jnp.where(lane == TOP_K + k, ids[k], table)
        table = jnp.where(lane == 2 * TOP_K + k, rank, table)
    return table, counts + jnp.sum(chosen, axis=0, keepdims=True)


def _xattn_out_kernel(x_ref, o_ref, wo_ref, g_ref, wr_ref, br_ref, cnt_ref,
                      x2_ref, xn_ref, rt_ref, cnt_out_ref, cnt_scr, *, n_exp):
    @pl.when(pl.program_id(0) == 0)
    def _():
        cnt_scr[...] = cnt_ref[...]

    x2 = x_ref[...] + _dot(o_ref[...].astype(BF16), wo_ref[...])
    x2_ref[...] = x2
    xn = _rms_f32(x2, g_ref[...])
    _store_slabs(xn_ref, xn)
    logits = _dot(xn.astype(BF16), wr_ref[...]) + br_ref[...]
    table, counts = _top_k_route(logits, cnt_scr[...], n_exp)
    rt_ref[...] = table
    cnt_scr[...] = counts
    cnt_out_ref[...] = counts


def _xattn_out(x, o, wo, g, wr, br, counts, *, n_exp, tm):
    n, d = x.shape
    assert d == SLAB_ROWS * LANES
    row = lambda i: (i, 0)
    fixed = lambda i: (0, 0)
    return pl.pallas_call(
        functools.partial(_xattn_out_kernel, n_exp=n_exp),
        grid=(n // tm,),
        in_specs=[pl.BlockSpec((tm, d), row), pl.BlockSpec((tm, o.shape[1]), row),
                  pl.BlockSpec(wo.shape, fixed), pl.BlockSpec((1, d), fixed),
                  pl.BlockSpec(wr.shape, fixed), pl.BlockSpec((1, LANES), fixed),
                  pl.BlockSpec((1, LANES), fixed)],
        out_specs=[pl.BlockSpec((tm, d), row), pl.BlockSpec((tm * SLAB_ROWS, LANES), row),
                   pl.BlockSpec((tm, LANES), row), pl.BlockSpec((1, LANES), fixed)],
        out_shape=[jax.ShapeDtypeStruct((n, d), F32), jax.ShapeDtypeStruct((n * SLAB_ROWS, LANES), F32),
                   jax.ShapeDtypeStruct((n, LANES), F32), jax.ShapeDtypeStruct((1, LANES), F32)],
        scratch_shapes=[pltpu.VMEM((1, LANES), F32)],
        compiler_params=_cparams("arbitrary"),
        name="xattn_out",
    )(x, o, wo, g, wr, br, counts)


def _slab(ref, row):
    return ref.at[pl.ds(pl.multiple_of(row * SLAB_ROWS, SLAB_ROWS), SLAB_ROWS)]


def _dispatch_kernel(dest_ref, x_ref, xs_in, xs_hbm, sem):
    del xs_in
    n_tok = dest_ref.shape[1] // TOP_K

    def slab_copies(t):
        src = _slab(x_ref, t)
        return [pltpu.make_async_copy(src, _slab(xs_hbm, dest_ref[0, t * TOP_K + k]), sem)
                for k in range(TOP_K)]

    def start(t, c):
        for cp in slab_copies(t):
            cp.start()
        return c

    def wait(t, c):
        for cp in slab_copies(t):
            cp.wait()
        return c

    lax.fori_loop(0, n_tok, start, 0, unroll=2)
    lax.fori_loop(0, n_tok, wait, 0, unroll=2)


def _dispatch(xn_slabs, dest, xs, *, tc):
    n = xn_slabs.shape[0] // SLAB_ROWS
    return pl.pallas_call(
        _dispatch_kernel,
        grid=(n // tc,),
        in_specs=[pl.BlockSpec((None, 1, tc * TOP_K), lambda i: (i, 0, 0), memory_space=pltpu.SMEM),
                  pl.BlockSpec((tc * SLAB_ROWS, LANES), lambda i: (i, 0)),
                  pl.BlockSpec(memory_space=pl.ANY)],
        out_specs=pl.BlockSpec(memory_space=pl.ANY),
        out_shape=jax.ShapeDtypeStruct(xs.shape, xs.dtype),
        scratch_shapes=[pltpu.SemaphoreType.DMA(())],
        input_output_aliases={2: 0},
        compiler_params=_cparams("arbitrary"),
        name="moe_dispatch",
    )(dest, xn_slabs, xs)


def _for_valid_rows(tile, n_used, valid_ref, tm, compute, zero_all):
    used = tile < n_used
    valid = valid_ref[tile]
    sub = tm // ROW_VARIANTS

    @pl.when(jnp.logical_not(used))
    def _():
        zero_all()

    for i in range(ROW_VARIANTS):
        lo, m = i * sub, (i + 1) * sub
        in_range = valid > lo if m == tm else jnp.logical_and(valid > lo, valid <= m)

        @pl.when(jnp.logical_and(used, in_range))
        def _(m=m):
            compute(m)


def _gate_up_kernel(be_ref, nblk_ref, valid_ref, x_ref, *refs):
    del be_ref
    wg_refs, wl_refs = refs[:WEIGHT_SPLITS], refs[WEIGHT_SPLITS:2 * WEIGHT_SPLITS]
    bg_ref, bl_ref, h_ref = refs[2 * WEIGHT_SPLITS:]
    tm = h_ref.shape[0]
    tc = h_ref.shape[1] // WEIGHT_SPLITS

    def zero_all():
        h_ref[...] = jnp.zeros_like(h_ref)

    def compute(m):
        x = _load_slabs(x_ref, m).astype(BF16)
        for c in range(WEIGHT_SPLITS):
            cols = slice(c * tc, (c + 1) * tc)
            gate = jnp.minimum(_dot(x, wg_refs[c][...].astype(BF16)) + bg_ref[:, cols], SWIGLU_LIMIT)
            lin = jnp.clip(_dot(x, wl_refs[c][...].astype(BF16)) + bl_ref[:, cols], -SWIGLU_LIMIT, SWIGLU_LIMIT)
            h_ref[:m, cols] = ((lin + 1.0) * gate * jax.nn.sigmoid(SWIGLU_ALPHA * gate)).astype(h_ref.dtype)
        if m < tm:
            h_ref[m:, :] = jnp.zeros((tm - m, h_ref.shape[1]), h_ref.dtype)

    _for_valid_rows(pl.program_id(1), nblk_ref[0], valid_ref, tm, compute, zero_all)


def _gate_up(xs, w_gu, b_gu, blk_e, nblk, valid, *, tm, tn):
    rows = xs.shape[0] // SLAB_ROWS
    n_exp, d, de2 = w_gu.shape
    de = de2 // 2
    nj = de // tn
    used = lambda j, r, be, nb, vl: (jnp.minimum(r, nb[0] - 1), 0)
    exp_col = lambda off: (lambda j, r, be, nb, vl: (be[jnp.minimum(r, nb[0] - 1)], 0, j + off))
    tc = tn // WEIGHT_SPLITS
    chunk = lambda half, c: (lambda j, r, be, nb, vl: (
        be[jnp.minimum(r, nb[0] - 1)], 0, (half * nj + j) * WEIGHT_SPLITS + c))
    w_specs = [pl.BlockSpec((None, d, tc), chunk(half, c)) for half in range(2) for c in range(WEIGHT_SPLITS)]
    grid_spec = pltpu.PrefetchScalarGridSpec(
        num_scalar_prefetch=3,
        grid=(nj, rows // tm),
        in_specs=[pl.BlockSpec((tm * SLAB_ROWS, LANES), used)] + w_specs
        + [pl.BlockSpec((None, 1, tn), exp_col(0)), pl.BlockSpec((None, 1, tn), exp_col(nj))],
        out_specs=pl.BlockSpec((tm, tn), lambda j, r, be, nb, vl: (r, j)),
    )
    b3 = b_gu.reshape(n_exp, 1, de2)
    return pl.pallas_call(
        _gate_up_kernel,
        grid_spec=grid_spec,
        out_shape=jax.ShapeDtypeStruct((rows, de), BF16),
        compiler_params=_cparams("arbitrary", "arbitrary"),
        name="moe_gate_up",
    )(blk_e, nblk, valid, xs, *([w_gu] * (2 * WEIGHT_SPLITS)), b3, b3)


def _down_kernel(be_ref, nblk_ref, valid_ref, h_ref, *refs):
    del be_ref
    w_refs = refs[:WEIGHT_SPLITS]
    b_ref, o_ref = refs[WEIGHT_SPLITS:]
    tm = h_ref.shape[0]
    tc = b_ref.shape[1] // WEIGHT_SPLITS
    slabs_per_chunk = tc // LANES

    def zero_all():
        o_ref[...] = jnp.zeros_like(o_ref)

    def compute(m):
        h = h_ref[:m, :]
        for c in range(WEIGHT_SPLITS):
            out = _dot(h, w_refs[c][...].astype(BF16)) + b_ref[:, c * tc:(c + 1) * tc]
            for s in range(slabs_per_chunk):
                o_ref[pl.ds(c * slabs_per_chunk + s, m, stride=SLAB_ROWS), :] = out[:, s * LANES:(s + 1) * LANES]
        if m < tm:
            o_ref[m * SLAB_ROWS:, :] = jnp.zeros(((tm - m) * SLAB_ROWS, LANES), o_ref.dtype)

    _for_valid_rows(pl.program_id(0), nblk_ref[0], valid_ref, tm, compute, zero_all)


def _down(hdn, w_down, b_down, blk_e, nblk, valid, *, tm):
    rows, de = hdn.shape
    n_exp, _, d = w_down.shape
    assert d == SLAB_ROWS * LANES
    used = lambda r, be, nb, vl: (jnp.minimum(r, nb[0] - 1), 0)
    expert = lambda r, be, nb, vl: (be[jnp.minimum(r, nb[0] - 1)], 0, 0)
    grid_spec = pltpu.PrefetchScalarGridSpec(
        num_scalar_prefetch=3,
        grid=(rows // tm,),
        in_specs=[pl.BlockSpec((tm, de), used)]
        + [pl.BlockSpec((None, de, d // WEIGHT_SPLITS),
                        lambda r, be, nb, vl, c=c: (be[jnp.minimum(r, nb[0] - 1)], 0, c))
           for c in range(WEIGHT_SPLITS)]
        + [pl.BlockSpec((None, 1, d), expert)],
        out_specs=pl.BlockSpec((tm * SLAB_ROWS, LANES), lambda r, be, nb, vl: (r, 0)),
    )
    return pl.pallas_call(
        _down_kernel,
        grid_spec=grid_spec,
        out_shape=jax.ShapeDtypeStruct((rows * SLAB_ROWS, LANES), F32),
        compiler_params=_cparams("arbitrary"),
        name="moe_down",
    )(blk_e, nblk, valid, hdn, *([w_down] * WEIGHT_SPLITS), b_down.reshape(n_exp, 1, d))


def _combine_kernel(pos_ref, x_ref, gate_ref, rows_hbm, g_ref, y_ref, buf, sem):
    tc = x_ref.shape[0]
    hc = tc // 2
    per_half = TOP_K * hc

    def slab_copy(half, i):
        return pltpu.make_async_copy(_slab(rows_hbm, pos_ref[0, half * per_half + i]),
                                     _slab(buf.at[half], i), sem.at[half])

    for half in range(2):
        lax.fori_loop(0, per_half, lambda i, c, half=half: (slab_copy(half, i).start(), c)[1], 0, unroll=8)
    for half in range(2):
        lax.fori_loop(0, per_half, lambda i, c, half=half: (slab_copy(half, i).wait(), c)[1], 0, unroll=8)
        rows = slice(half * hc, (half + 1) * hc)
        x3 = x_ref[rows, :]
        for k in range(TOP_K):
            x3 = x3 + gate_ref[rows, k:k + 1] * _load_slabs(buf.at[half], hc, first_row=k * hc * SLAB_ROWS)
        y_ref[rows, :] = _rms_f32(x3, g_ref[...])


def _combine(x, gates, rows, pos, g, *, tc):
    n, d = x.shape
    return pl.pallas_call(
        _combine_kernel,
        grid=(n // tc,),
        in_specs=[pl.BlockSpec((None, 1, TOP_K * tc), lambda i: (i, 0, 0), memory_space=pltpu.SMEM),
                  pl.BlockSpec((tc, d), lambda i: (i, 0)),
                  pl.BlockSpec((tc, TOP_K), lambda i: (i, 0)),
                  pl.BlockSpec(memory_space=pl.ANY),
                  pl.BlockSpec((1, d), lambda i: (0, 0))],
        out_specs=pl.BlockSpec((tc, d), lambda i: (i, 0)),
        out_shape=jax.ShapeDtypeStruct((n, d), F32),
        scratch_shapes=[pltpu.VMEM((2, TOP_K * (tc // 2) * SLAB_ROWS, LANES), F32),
                        pltpu.SemaphoreType.DMA((2,))],
        compiler_params=_cparams("arbitrary"),
        name="moe_combine",
    )(pos, x, gates, rows, g)


def _row_layout(table, counts, n_exp, tm):
    n_tok = table.shape[0]
    gates = table[:, :TOP_K]
    expert = table[:, TOP_K:2 * TOP_K].astype(jnp.int32)
    rank = table[:, 2 * TOP_K:3 * TOP_K].astype(jnp.int32)
    counts = counts.astype(jnp.int32)
    padded = (counts + tm - 1) // tm * tm
    pad_ends = jnp.cumsum(padded)
    pad_starts = pad_ends - padded
    is_e = expert[:, :, None] == jnp.arange(n_exp, dtype=jnp.int32)[None, None, :]
    dest = rank + jnp.sum(jnp.where(is_e, pad_starts[None, None, :], 0), axis=-1)
    n_blocks = -(-n_tok * TOP_K // tm) + n_exp
    tile_start = jnp.arange(n_blocks, dtype=jnp.int32) * tm
    blk_e = jnp.minimum(jnp.sum((pad_ends[None, :] <= tile_start[:, None]).astype(jnp.int32), axis=1),
                        n_exp - 1)
    nblk = (pad_ends[-1] // tm).reshape(1)
    is_blk_e = blk_e[:, None] == jnp.arange(n_exp, dtype=jnp.int32)[None, :]
    real_end = jnp.sum(jnp.where(is_blk_e, (pad_starts + counts)[None, :], 0), axis=1)
    valid = jnp.clip(real_end - tile_start, 0, tm).astype(jnp.int32)
    return gates, dest, blk_e, nblk, valid, n_blocks


def _moe_final(x2_groups, xn_groups, table_groups, counts, w_gu, b_gu, w_down, b_down, g_final):
    n_exp = w_gu.shape[0]
    tm = min(EXPERT_TILE, TOP_K * sum(x.shape[0] for x in x2_groups))
    gates, dest, blk_e, nblk, valid, n_blocks = _row_layout(
        jnp.concatenate(table_groups, axis=0), counts[0, :n_exp], n_exp, tm)
    xs = jnp.zeros((n_blocks * tm * SLAB_ROWS, LANES), F32)
    start = 0
    for xn in xn_groups:
        n = xn.shape[0] // SLAB_ROWS
        tc = min(COMBINE_TILE, n)
        xs = _dispatch(xn, dest[start:start + n].reshape(n // tc, 1, tc * TOP_K), xs, tc=tc)
        start += n
    hdn = _gate_up(xs, w_gu, b_gu, blk_e, nblk, valid, tm=tm, tn=min(1024, w_gu.shape[2] // 2))
    rows = _down(hdn, w_down, b_down, blk_e, nblk, valid, tm=tm)
    outs, start = [], 0
    for x2 in x2_groups:
        n = x2.shape[0]
        tc = min(COMBINE_TILE, n)
        pos = dest[start:start + n].reshape(n // tc, 2, tc // 2, TOP_K).transpose(0, 1, 3, 2)
        outs.append(_combine(x2, gates[start:start + n], rows, pos.reshape(n // tc, 1, TOP_K * tc),
                             g_final, tc=tc))
        start += n
    return outs


def _layer_common(x, attn, pool, wa, wp, g_x, wq, mk, mv, rows_per_mem, xattn_tile,
                  wo, g_f, wr, br, counts, n_exp):
    n = x.shape[0]
    x1, qx = _mix_out(x, attn, pool, wa, wp, g_x, wq, tm=min(256, n))
    o = _xattn(qx, mk, mv, rows_per_mem=rows_per_mem, n_mem=mk.shape[0] * rows_per_mem // n,
               tm=xattn_tile)
    return _xattn_out(x1, o, wo, g_f, wr, br, counts, n_exp=n_exp, tm=min(256, n))


def kernel(x_prompt, x_sample, mem_prompt, cache_k, cache_v, cache_logf, cache_mem_k, cache_mem_v,
           state_pool, page_table, g_mix, w_in, b_forget, w_pool, s_pool, w_out, g_xattn, g_mem,
           w_xq, w_xk, w_xv, w_xo, g_ffn, w_router, b_router, w_gu, b_gu, w_down, b_down, g_final):
    depth = w_in.shape[0]
    bp, tp, d = x_prompt.shape
    bs, ts, _ = x_sample.shape
    n_pool, page, heads, hd = cache_k.shape[1:]
    fw = heads * hd
    pw = state_pool.shape[-1]
    n_mem = mem_prompt.shape[1]
    xw = w_xq.shape[-1]
    n_exp = w_router.shape[-1]
    past_len = page_table.shape[1] * page
    assert hd == HEAD_DIM and page == PAGE_SIZE and heads == 8 and pw == fw
    assert w_in.shape[-1] == 3 * fw + heads + pw and state_pool.shape[2] == POOL_STATE
    assert depth == 1, "the experts of all groups are evaluated together after the last layer"

    xp = x_prompt.reshape(bp * tp, d)
    xs = x_sample.reshape(bs * ts, d)
    row = lambda a: a.reshape(1, -1)
    outs = {k: [] for k in ("kp", "vp", "lfp", "pp", "mkp", "mvp", "ks", "vs", "lfs", "ps")}
    for l in range(depth):
        w = w_in[l]
        w4 = jnp.concatenate([w[:, :3 * fw], w[:, 3 * fw + heads:]], axis=1).astype(BF16)
        wf = jnp.pad(w[:, 3 * fw:3 * fw + heads], ((0, 0), (0, LANES - heads))).astype(BF16)
        bfp = jnp.pad(b_forget[l], (0, LANES - heads)).reshape(1, LANES)
        wpool = w_pool[l].astype(BF16)
        spool = row(s_pool[l])
        wa = w_out[l, :fw].astype(BF16)
        wp = w_out[l, fw:].astype(BF16)
        wq = w_xq[l].astype(BF16)
        wkv = jnp.concatenate([w_xk[l], w_xv[l]], axis=1).astype(BF16)
        wo = w_xo[l].astype(BF16)
        wr = jnp.pad(w_router[l], ((0, 0), (0, LANES - n_exp))).astype(BF16)
        br = jnp.pad(b_router[l], (0, LANES - n_exp)).reshape(1, LANES)

        tm = min(512, tp)
        q, k, v, u, kb, vb, lf, ct = _in_proj(xp, row(g_mix[l]), w4, wf, bfp, seq_len=tp, heads=heads, tm=tm)
        attn = _fox_prompt(q, kb, vb, ct, batch=bp, seq_len=tp, heads=heads, tq=tm,
                           heads_per_step=FLASH_HEADS_PER_STEP)
        pool = _pool_prompt(u, wpool, spool, seq_len=tp, tm=tm)
        outs["kp"].append(k.reshape(bp, tp, heads, hd))
        outs["vp"].append(v.reshape(bp, tp, heads, hd))
        outs["lfp"].append(lf.reshape(bp, tp, heads))
        outs["pp"].append(u.reshape(bp, tp, pw)[:, tp - POOL_STATE:])
        mk, mv = _mem_kv(mem_prompt.reshape(bp * n_mem, d), row(g_mem[l]), wkv, tm=min(256, bp * n_mem))
        outs["mkp"].append(mk.reshape(bp, n_mem, xw // HEAD_DIM, HEAD_DIM))
        outs["mvp"].append(mv.reshape(bp, n_mem, xw // HEAD_DIM, HEAD_DIM))
        x2p, xnp_, rtp, counts = _layer_common(xp, attn, pool, wa, wp, row(g_xattn[l]), wq, mk, mv, tp, tm,
                                               wo, row(g_ffn[l]), wr, br, jnp.zeros((1, LANES), F32), n_exp)

        ns = bs * ts
        tms = min(512, ns)
        q, k, v, u, _, _, lf, ct = _in_proj(xs, row(g_mix[l]), w4, wf, bfp, seq_len=ts, heads=heads, tm=tms)
        c_new = ct.reshape(ns // tms, heads, tms // ts, ts).transpose(0, 2, 3, 1).reshape(bs, 1, ts * heads)
        c_new = jnp.pad(c_new, ((0, 0), (0, 0), (0, LANES - ts * heads)))
        scan = _logf_scan(cache_logf[l].reshape(n_pool, page * heads), heads=heads,
                          tm=_divisor_tile(n_pool, 512))
        attn = _fox_decode(q, k.reshape(bs, ts * heads, hd), v.reshape(bs, ts * heads, hd), c_new,
                           cache_k[l].reshape(n_pool, page * heads, hd),
                           cache_v[l].reshape(n_pool, page * heads, hd),
                           scan, page_table, heads=heads, n_new=ts,
                           pages_per_step=_divisor_tile(page_table.shape[1], DECODE_PAGES_PER_STEP, 1))
        u3 = u.reshape(bs, ts, pw)
        pool = _pool_sample(u3.transpose(1, 0, 2), state_pool[l].transpose(1, 0, 2), wpool, spool,
                            past_len=past_len)
        pool = pool.transpose(1, 0, 2).reshape(ns, pw)
        outs["ks"].append(k.reshape(bs, ts, heads, hd))
        outs["vs"].append(v.reshape(bs, ts, heads, hd))
        outs["lfs"].append(lf.reshape(bs, ts, heads))
        outs["ps"].append(jnp.concatenate([state_pool[l], u3], axis=1)[:, ts:])
        mks = cache_mem_k[l].reshape(bs * n_mem, xw)
        mvs = cache_mem_v[l].reshape(bs * n_mem, xw)
        x2s, xns, rts, counts = _layer_common(xs, attn, pool, wa, wp, row(g_xattn[l]), wq, mks, mvs, ts,
                                              ts * _divisor_tile(bs, XATTN_SEQS_PER_STEP, 1),
                                              wo, row(g_ffn[l]), wr, br, counts, n_exp)

        yp, ys = _moe_final([x2p, x2s], [xnp_, xns], [rtp, rts], counts, w_gu[l], b_gu[l], w_down[l],
                            b_down[l], row(g_final))

    st = lambda name: jnp.stack(outs[name])
    return (yp.reshape(bp, tp, d), ys.reshape(bs, ts, d),
            st("kp"), st("vp"), st("lfp"), st("pp"), st("mkp"), st("mvp"),
            st("ks"), st("vs"), st("lfs"), st("ps"))
```

```python
import functools

import jax
import jax.numpy as jnp
from jax import lax
from jax.experimental import pallas as pl
from jax.experimental.pallas import tpu as pltpu

F32 = jnp.float32
BF16 = jnp.bfloat16

HEAD_DIM = 128
PAGE_SIZE = 128
POOL_WINDOWS = (2, 4, 8, 16)
POOL_STATE = max(POOL_WINDOWS) - 1
POOL_HALO = 16
TOP_K = 4
SWIGLU_ALPHA = 1.702
SWIGLU_LIMIT = 7.0
NORM_EPS = 1e-5
LANES = 128
BF16_ROWS = 16
SLAB_ROWS = 16
SCAN_ROWS = 8
VMEM_LIMIT = 56 * 1024 * 1024
NEG_INF = float("-inf")
EXPERT_TILE = 512
ROW_VARIANTS = 4
COMBINE_TILE = 128


DECODE_PAGES_PER_STEP = 8
FLASH_HEADS_PER_STEP = 4
XATTN_SEQS_PER_STEP = 8


def _divisor_tile(n, cap, multiple=8):
    for t in range(min(cap, n), 0, -1):
        if n % t == 0 and t % multiple == 0:
            return t
    raise ValueError(f"no tile for {n}")


def _cparams(*sem):
    return pltpu.CompilerParams(dimension_semantics=sem, vmem_limit_bytes=VMEM_LIMIT)


def _rms_f32(x, g):
    return x * lax.rsqrt(jnp.mean(x * x, axis=-1, keepdims=True) + NORM_EPS) * g


def _dot(a, b):
    return jnp.dot(a, b, preferred_element_type=F32)


def _dot_nt(a, b):
    return lax.dot_general(a, b, (((1,), (1,)), ((), ())), preferred_element_type=F32)


def _pad_rows(x):
    rows = x.shape[0]
    if rows >= BF16_ROWS:
        return x
    return jnp.concatenate([x, jnp.zeros((BF16_ROWS - rows,) + x.shape[1:], x.dtype)], axis=0)


def _split3(x):
    hi = x.astype(BF16)
    r = x - hi.astype(F32)
    mid = r.astype(BF16)
    lo = (r - mid.astype(F32)).astype(BF16)
    return hi, mid, lo


def _lane_cumsum(x, seg):
    n = x.shape[-1]
    r = lax.broadcasted_iota(jnp.int32, (n, n), 0)
    c = lax.broadcasted_iota(jnp.int32, (n, n), 1)
    keep = r <= c
    if seg < n:
        keep = jnp.logical_and(keep, (r // seg) == (c // seg))
    tri = jnp.where(keep, 1.0, 0.0).astype(BF16)
    rows = x.shape[0]
    hi, mid, lo = _split3(_pad_rows(x))
    return (_dot(hi, tri) + _dot(mid, tri) + _dot(lo, tri))[:rows]


def _log_sigmoid(z):
    return jnp.minimum(z, 0.0) - jnp.log1p(jnp.exp(-jnp.abs(z)))


def _in_proj_kernel(x_ref, g_ref, w_ref, wf_ref, bf_ref,
                    q_ref, k_ref, v_ref, u_ref, kb_ref, vb_ref, lf_ref, ct_ref,
                    xn_scr, carry_scr, *, tiles_per_seq, seg, heads):
    i = pl.program_id(0)
    j = pl.program_id(1)

    @pl.when(j == 0)
    def _():
        xn = _rms_f32(x_ref[...], g_ref[...]).astype(BF16)
        xn_scr[...] = xn
        lf = _log_sigmoid(_dot(xn, wf_ref[...]) + bf_ref[...])
        lf_ref[...] = lf[:, :heads]
        lft = lf.T[:8]
        c = _lane_cumsum(lft, seg)

        @pl.when(i % tiles_per_seq == 0)
        def _():
            carry_scr[...] = jnp.zeros_like(carry_scr)

        c = c + carry_scr[:, :1]
        ct_ref[0] = c
        carry_scr[...] = jnp.broadcast_to(c[:, -1:], carry_scr.shape)

    y = _dot(xn_scr[...], w_ref[...])

    @pl.when(j == 0)
    def _():
        q_ref[...] = y

    @pl.when(j == 1)
    def _():
        k_ref[...] = y
        kb_ref[...] = y.astype(BF16)

    @pl.when(j == 2)
    def _():
        v_ref[...] = y
        vb_ref[...] = y.astype(BF16)

    @pl.when(j == 3)
    def _():
        u_ref[...] = y


def _in_proj(x, g, w4, wf, bfp, *, seq_len, heads, tm):
    n, d = x.shape
    width = w4.shape[1] // 4
    if seq_len >= tm:
        tiles_per_seq, seg = seq_len // tm, tm
    else:
        tiles_per_seq, seg = 1, seq_len
    n_tiles = n // tm
    row = lambda i, j: (i, 0)
    big = jax.ShapeDtypeStruct((n, width), F32)
    bigb = jax.ShapeDtypeStruct((n, width), BF16)
    return pl.pallas_call(
        functools.partial(_in_proj_kernel, tiles_per_seq=tiles_per_seq, seg=seg, heads=heads),
        grid=(n_tiles, 4),
        in_specs=[
            pl.BlockSpec((tm, d), row),
            pl.BlockSpec((1, d), lambda i, j: (0, 0)),
            pl.BlockSpec((d, width), lambda i, j: (0, j)),
            pl.BlockSpec((d, LANES), lambda i, j: (0, 0)),
            pl.BlockSpec((1, LANES), lambda i, j: (0, 0)),
        ],
        out_specs=[
            pl.BlockSpec((tm, width), row), pl.BlockSpec((tm, width), row),
            pl.BlockSpec((tm, width), row), pl.BlockSpec((tm, width), row),
            pl.BlockSpec((tm, width), row), pl.BlockSpec((tm, width), row),
            pl.BlockSpec((tm, heads), row),
            pl.BlockSpec((1, 8, tm), lambda i, j: (i, 0, 0)),
        ],
        out_shape=[big, big, big, big, bigb, bigb,
                   jax.ShapeDtypeStruct((n, heads), F32),
                   jax.ShapeDtypeStruct((n_tiles, 8, tm), F32)],
        scratch_shapes=[pltpu.VMEM((tm, d), BF16), pltpu.VMEM((8, LANES), F32)],
        compiler_params=_cparams("arbitrary", "arbitrary"),
        name="in_proj",
    )(x, g, w4, wf, bfp)


def _fox_flash_kernel(qi_ref, ki_ref, q_ref, k_ref, v_ref, ct_ref, o_ref, m_scr, l_scr, acc_scr,
                      *, scale, heads_per_step):
    hp = pl.program_id(1)
    t = pl.program_id(2)
    qi = qi_ref[t]
    ki = ki_ref[t]

    @pl.when(ki == 0)
    def _():
        m_scr[...] = jnp.full_like(m_scr, NEG_INF)
        l_scr[...] = jnp.zeros_like(l_scr)
        acc_scr[...] = jnp.zeros_like(acc_scr)

    def step(diagonal):
        for hh in range(heads_per_step):
            sl = slice(hh * HEAD_DIM, (hh + 1) * HEAD_DIM)
            q = (q_ref[:, sl] * scale).astype(BF16)
            s = _dot_nt(q, k_ref[:, sl]) - ct_ref[0, pl.ds(hp * heads_per_step + hh, 1), :]
            if diagonal:
                r = lax.broadcasted_iota(jnp.int32, s.shape, 0)
                c = lax.broadcasted_iota(jnp.int32, s.shape, 1)
                s = jnp.where(c <= r, s, NEG_INF)
            m_prev = m_scr[hh]
            m_new = jnp.maximum(m_prev, jnp.max(s, axis=-1, keepdims=True))
            alpha = jnp.exp(m_prev - m_new)
            p = jnp.exp(s - m_new[:, :1])
            l_new = alpha * l_scr[hh] + jnp.sum(p, axis=-1, keepdims=True)
            acc = alpha * acc_scr[hh] + _dot(p.astype(BF16), v_ref[:, sl])
            if diagonal:
                o_ref[:, sl] = (acc / l_new).astype(o_ref.dtype)
            else:
                l_scr[hh] = l_new
                acc_scr[hh] = acc
                m_scr[hh] = m_new

    @pl.when(ki < qi)
    def _():
        step(False)

    @pl.when(ki == qi)
    def _():
        step(True)


def _fox_prompt(q, kb, vb, ct, *, batch, seq_len, heads, tq, heads_per_step):
    n, width = q.shape
    nq = seq_len // tq
    hw = heads_per_step * HEAD_DIM
    pairs = [(qi, ki) for qi in range(nq) for ki in range(qi + 1)]
    qi_tab = jnp.array([p[0] for p in pairs], jnp.int32)
    ki_tab = jnp.array([p[1] for p in pairs], jnp.int32)
    q_blk = lambda b, hp, t, qt, kt: (b * nq + qt[t], hp)
    kv_blk = lambda b, hp, t, qt, kt: (b * nq + kt[t], hp)
    grid_spec = pltpu.PrefetchScalarGridSpec(
        num_scalar_prefetch=2,
        grid=(batch, heads // heads_per_step, len(pairs)),
        in_specs=[
            pl.BlockSpec((tq, hw), q_blk),
            pl.BlockSpec((tq, hw), kv_blk),
            pl.BlockSpec((tq, hw), kv_blk),
            pl.BlockSpec((1, 8, tq), lambda b, hp, t, qt, kt: (b * nq + kt[t], 0, 0)),
        ],
        out_specs=pl.BlockSpec((tq, hw), q_blk),
        scratch_shapes=[pltpu.VMEM((heads_per_step, tq, LANES), F32),
                        pltpu.VMEM((heads_per_step, tq, LANES), F32),
                        pltpu.VMEM((heads_per_step, tq, HEAD_DIM), F32)],
    )
    return pl.pallas_call(
        functools.partial(_fox_flash_kernel, scale=HEAD_DIM ** -0.5, heads_per_step=heads_per_step),
        grid_spec=grid_spec,
        out_shape=jax.ShapeDtypeStruct((n, width), BF16),
        compiler_params=_cparams("arbitrary", "arbitrary", "arbitrary"),
        name="fox_prompt",
    )(qi_tab, ki_tab, q, kb, vb, ct)


def _logf_scan_kernel(x_ref, o_ref, *, heads):
    n = x_ref.shape[1]
    r = lax.broadcasted_iota(jnp.int32, (n, n), 0)
    c = lax.broadcasted_iota(jnp.int32, (n, n), 1)
    same_head = jnp.bitwise_and(r, heads - 1) == jnp.bitwise_and(c, heads - 1)
    m_tot = jnp.where(same_head, 1.0, 0.0).astype(BF16)
    m_cum = jnp.where(jnp.logical_and(same_head, r <= c), 1.0, 0.0).astype(BF16)
    hi, mid, lo = _split3(x_ref[...])
    o_ref[:, :n] = _dot(hi, m_cum) + _dot(mid, m_cum) + _dot(lo, m_cum)
    o_ref[:, n:] = _dot(hi, m_tot) + _dot(mid, m_tot) + _dot(lo, m_tot)


def _logf_scan(logf_pages, *, heads, tm):
    n_pool, n = logf_pages.shape
    return pl.pallas_call(
        functools.partial(_logf_scan_kernel, heads=heads),
        grid=(n_pool // tm,),
        in_specs=[pl.BlockSpec((tm, n), lambda i: (i, 0))],
        out_specs=pl.BlockSpec((tm, 2 * n), lambda i: (i, 0)),
        out_shape=jax.ShapeDtypeStruct((n_pool, 2 * n), F32),
        compiler_params=_cparams("arbitrary"),
        name="logf_scan",
    )(logf_pages)


def _fox_decode_kernel(pt_ref, q_ref, *refs, scale, heads, n_new, pages_per_step):
    pps = pages_per_step
    k_refs, v_refs, scan_refs = refs[:pps], refs[pps:2 * pps], refs[2 * pps:3 * pps]
    kn_ref, vn_ref, cn_ref, o_ref, m_scr, l_scr, acc_scr, carry_scr = refs[3 * pps:]
    step = pl.program_id(1)
    last = pl.num_programs(1) - 1
    rows = heads * n_new
    page_rows = PAGE_SIZE * heads

    @pl.when(step == 0)
    def _():
        m_scr[...] = jnp.full_like(m_scr, NEG_INF)
        l_scr[...] = jnp.zeros_like(l_scr)
        acc_scr[...] = jnp.zeros_like(acc_scr)
        carry_scr[...] = jnp.zeros_like(carry_scr)

    qb = jnp.concatenate([q_ref[:, h * HEAD_DIM:(h + 1) * HEAD_DIM] for h in range(heads)], axis=0)
    qb = (qb * scale).astype(BF16)

    def update(s, pv):
        m_prev = m_scr[...]
        m_new = jnp.maximum(m_prev, jnp.max(s, axis=-1, keepdims=True))
        alpha = jnp.exp(m_prev - m_new)
        p = jnp.exp(s - m_new[:, :1])
        l_scr[...] = alpha * l_scr[...] + jnp.sum(p, axis=-1, keepdims=True)
        acc_scr[...] = alpha * acc_scr[...] + pv(p.astype(BF16))
        m_scr[...] = m_new

    r = lax.broadcasted_iota(jnp.int32, (rows, page_rows), 0)
    c = lax.broadcasted_iota(jnp.int32, (rows, page_rows), 1)
    head_ok = jnp.bitwise_and(c, heads - 1) == r // n_new
    carry = carry_scr[...]
    parts = []
    for i in range(pps):
        s = _dot_nt(qb, k_refs[i][0].astype(BF16))
        page = pt_ref[(pl.program_id(0) * pl.num_programs(1) + step) * pps + i]
        scan = scan_refs[i][pl.ds(page % SCAN_ROWS, 1), :]
        parts.append(jnp.where(head_ok, s - (scan[:, :page_rows] + carry), NEG_INF))
        carry = carry + scan[:, page_rows:]
    carry_scr[...] = carry

    def page_values(p):
        out = _dot(p[:, :page_rows], v_refs[0][0].astype(BF16))
        for i in range(1, pps):
            out = out + _dot(p[:, i * page_rows:(i + 1) * page_rows], v_refs[i][0].astype(BF16))
        return out

    update(jnp.concatenate(parts, axis=1), page_values)

    @pl.when(step == last)
    def _():
        pad = jnp.zeros((LANES - rows, HEAD_DIM), F32)
        rn = lax.broadcasted_iota(jnp.int32, (rows, LANES), 0)
        cn = lax.broadcasted_iota(jnp.int32, (rows, LANES), 1)
        ok = jnp.logical_and(jnp.bitwise_and(cn, heads - 1) == rn // n_new,
                             cn // heads <= jnp.bitwise_and(rn, n_new - 1))
        s = _dot_nt(qb, jnp.concatenate([kn_ref[0], pad], axis=0).astype(BF16))
        s = jnp.where(ok, s - (cn_ref[0] + carry[:, :LANES]), NEG_INF)
        vn = jnp.concatenate([vn_ref[0], pad], axis=0).astype(BF16)
        update(s, lambda p: _dot(p, vn))
        out = acc_scr[...] / l_scr[...]
        for h in range(heads):
            o_ref[:, h * HEAD_DIM:(h + 1) * HEAD_DIM] = out[h * n_new:(h + 1) * n_new]


def _fox_decode(q, k_new, v_new, c_new, kc, vc, scan, page_table, *, heads, n_new, pages_per_step):
    n, width = q.shape
    n_seq, n_pages = page_table.shape
    pps = pages_per_step
    page_rows = PAGE_SIZE * heads
    seq = lambda b, p, pt: (b, 0, 0)
    page = lambda i: (lambda b, p, pt: (pt[b * n_pages + p * pps + i], 0, 0))
    kv_specs = [pl.BlockSpec((1, page_rows, HEAD_DIM), page(i)) for i in range(pps)]
    scan_page = lambda i: (lambda b, p, pt: (pt[b * n_pages + p * pps + i] // SCAN_ROWS, 0))
    scan_specs = [pl.BlockSpec((SCAN_ROWS, 2 * page_rows), scan_page(i)) for i in range(pps)]
    grid_spec = pltpu.PrefetchScalarGridSpec(
        num_scalar_prefetch=1,
        grid=(n_seq, n_pages // pps),
        in_specs=[pl.BlockSpec((n_new, width), lambda b, p, pt: (b, 0))]
        + kv_specs + kv_specs + scan_specs
        + [pl.BlockSpec((1, n_new * heads, HEAD_DIM), seq), pl.BlockSpec((1, n_new * heads, HEAD_DIM), seq),
           pl.BlockSpec((1, 1, LANES), seq)],
        out_specs=pl.BlockSpec((n_new, width), lambda b, p, pt: (b, 0)),
        scratch_shapes=[pltpu.VMEM((heads * n_new, LANES), F32), pltpu.VMEM((heads * n_new, LANES), F32),
                        pltpu.VMEM((heads * n_new, HEAD_DIM), F32), pltpu.VMEM((1, page_rows), F32)],
    )
    return pl.pallas_call(
        functools.partial(_fox_decode_kernel, scale=HEAD_DIM ** -0.5, heads=heads, n_new=n_new,
                          pages_per_step=pps),
        grid_spec=grid_spec,
        out_shape=jax.ShapeDtypeStruct((n, width), F32),
        compiler_params=_cparams("arbitrary", "arbitrary"),
        name="fox_decode",
    )(page_table.reshape(-1), q, *([kc] * pps), *([vc] * pps), *([scan] * pps), k_new, v_new, c_new)


def _pool_prompt_kernel(u_ref, halo_ref, w_ref, s_ref, o_ref, *, tiles_per_seq, gd):
    tm = u_ref.shape[0]
    tile_in_seq = pl.program_id(0) % tiles_per_seq
    pos = tile_in_seq * tm + lax.broadcasted_iota(jnp.int32, (tm, 1), 0)
    for g, w in enumerate(POOL_WINDOWS):
        sl = slice(g * gd, (g + 1) * gd)
        u = u_ref[:, sl]
        halo = jnp.where(tile_in_seq > 0, halo_ref[:, sl], 0.0)
        acc = jnp.concatenate([halo, u], axis=0)
        k = 1
        while k < w:
            acc = acc + pltpu.roll(acc, k, axis=0)
            k *= 2
        cnt = jnp.minimum(pos + 1, w).astype(F32)
        d = acc[POOL_HALO:] / cnt - u
        o_ref[:, sl] = (_dot(d.astype(BF16), w_ref[g]) * s_ref[:, sl]).astype(o_ref.dtype)


def _pool_prompt(u, w_pool, s_pool, *, seq_len, tm):
    n, pw = u.shape
    gd = pw // len(POOL_WINDOWS)
    halo_blocks = tm // POOL_HALO
    return pl.pallas_call(
        functools.partial(_pool_prompt_kernel, tiles_per_seq=seq_len // tm, gd=gd),
        grid=(n // tm,),
        in_specs=[
            pl.BlockSpec((tm, pw), lambda i: (i, 0)),
            pl.BlockSpec((POOL_HALO, pw), lambda i: (jnp.maximum(i * halo_blocks - 1, 0), 0)),
            pl.BlockSpec(w_pool.shape, lambda i: (0, 0, 0)),
            pl.BlockSpec((1, pw), lambda i: (0, 0)),
        ],
        out_specs=pl.BlockSpec((tm, pw), lambda i: (i, 0)),
        out_shape=jax.ShapeDtypeStruct((n, pw), BF16),
        compiler_params=_cparams("arbitrary"),
        name="pool_prompt",
    )(u, u, w_pool, s_pool)


def _pool_sample_kernel(u_ref, st_ref, w_ref, s_ref, o_ref, *, gd, past_len):
    n_new = u_ref.shape[0]

    def ext(j, sl):
        return st_ref[j, :, sl] if j < POOL_STATE else u_ref[j - POOL_STATE, :, sl]

    for g, w in enumerate(POOL_WINDOWS):
        sl = slice(g * gd, (g + 1) * gd)
        for t in range(n_new):
            cur = ext(POOL_STATE + t, sl)
            acc = cur
            for j in range(1, w):
                acc = acc + ext(POOL_STATE + t - j, sl)
            d = acc / float(min(past_len + t + 1, w)) - cur
            o_ref[t, :, sl] = (_dot(d.astype(BF16), w_ref[g]) * s_ref[:, sl]).astype(o_ref.dtype)


def _pool_sample(u_t, st_t, w_pool, s_pool, *, past_len):
    n_new, n_seq, pw = u_t.shape
    gd = pw // len(POOL_WINDOWS)
    return pl.pallas_call(
        functools.partial(_pool_sample_kernel, gd=gd, past_len=past_len),
        out_shape=jax.ShapeDtypeStruct((n_new, n_seq, pw), BF16),
        compiler_params=pltpu.CompilerParams(vmem_limit_bytes=VMEM_LIMIT),
        name="pool_sample",
    )(u_t, st_t, w_pool, s_pool)


def _mix_out_kernel(x_ref, a_ref, p_ref, wa_ref, wp_ref, g_ref, wq_ref, x1_ref, qx_ref):
    x1 = x_ref[...] + _dot(a_ref[...].astype(BF16), wa_ref[...]) + _dot(p_ref[...], wp_ref[...])
    x1_ref[...] = x1
    qx_ref[...] = _dot(_rms_f32(x1, g_ref[...]).astype(BF16), wq_ref[...])


def _mix_out(x, attn, pool, wa, wp, g, wq, *, tm):
    n, d = x.shape
    xw = wq.shape[1]
    row = lambda i: (i, 0)
    fixed = lambda i: (0, 0)
    return pl.pallas_call(
        _mix_out_kernel,
        grid=(n // tm,),
        in_specs=[
            pl.BlockSpec((tm, d), row), pl.BlockSpec((tm, attn.shape[1]), row),
            pl.BlockSpec((tm, pool.shape[1]), row),
            pl.BlockSpec(wa.shape, fixed), pl.BlockSpec(wp.shape, fixed),
            pl.BlockSpec((1, d), fixed), pl.BlockSpec(wq.shape, fixed),
        ],
        out_specs=[pl.BlockSpec((tm, d), row), pl.BlockSpec((tm, xw), row)],
        out_shape=[jax.ShapeDtypeStruct((n, d), F32), jax.ShapeDtypeStruct((n, xw), F32)],
        compiler_params=_cparams("arbitrary"),
        name="mix_out",
    )(x, attn, pool, wa, wp, g, wq)


def _mem_kv_kernel(m_ref, g_ref, w_ref, k_ref, v_ref):
    y = _dot(_rms_f32(m_ref[...], g_ref[...]).astype(BF16), w_ref[...])
    xw = k_ref.shape[1]
    k_ref[...] = y[:, :xw]
    v_ref[...] = y[:, xw:]


def _mem_kv(mem, g, wkv, *, tm):
    n, d = mem.shape
    xw = wkv.shape[1] // 2
    out = jax.ShapeDtypeStruct((n, xw), F32)
    return pl.pallas_call(
        _mem_kv_kernel,
        grid=(n // tm,),
        in_specs=[pl.BlockSpec((tm, d), lambda i: (i, 0)), pl.BlockSpec((1, d), lambda i: (0, 0)),
                  pl.BlockSpec(wkv.shape, lambda i: (0, 0))],
        out_specs=[pl.BlockSpec((tm, xw), lambda i: (i, 0))] * 2,
        out_shape=[out, out],
        compiler_params=_cparams("arbitrary"),
        name="mem_kv",
    )(mem, g, wkv)


def _xattn_kernel(q_ref, k_ref, v_ref, o_ref, *, scale, heads, groups):
    tq = q_ref.shape[0] // groups
    n_mem = k_ref.shape[0] // groups
    for g in range(groups):
        rows = slice(g * tq, (g + 1) * tq)
        mem = slice(g * n_mem, (g + 1) * n_mem)
        for h in range(heads):
            sl = slice(h * HEAD_DIM, (h + 1) * HEAD_DIM)
            q = _pad_rows(q_ref[rows, sl] * scale).astype(BF16)
            s = _dot_nt(q, k_ref[mem, sl].astype(BF16))
            p = jnp.exp(s - jnp.max(s, axis=-1, keepdims=True))
            o = _dot(p.astype(BF16), v_ref[mem, sl].astype(BF16)) / jnp.sum(p, axis=-1, keepdims=True)
            o_ref[rows, sl] = o[:tq]


def _xattn(q, mk, mv, *, rows_per_mem, n_mem, tm):
    n, xw = q.shape
    if tm <= rows_per_mem:
        groups, tiles_per_mem = 1, rows_per_mem // tm
        mem = lambda i: (i // tiles_per_mem, 0)
    else:
        groups = tm // rows_per_mem
        mem = lambda i: (i, 0)
    return pl.pallas_call(
        functools.partial(_xattn_kernel, scale=HEAD_DIM ** -0.5, heads=xw // HEAD_DIM, groups=groups),
        grid=(n // tm,),
        in_specs=[pl.BlockSpec((tm, xw), lambda i: (i, 0)),
                  pl.BlockSpec((groups * n_mem, xw), mem), pl.BlockSpec((groups * n_mem, xw), mem)],
        out_specs=pl.BlockSpec((tm, xw), lambda i: (i, 0)),
        out_shape=jax.ShapeDtypeStruct((n, xw), F32),
        compiler_params=_cparams("arbitrary"),
        name="xattn",
    )(q, mk, mv)


def _store_slabs(ref, x):
    m = x.shape[0]
    for s in range(SLAB_ROWS):
        ref[pl.ds(s, m, stride=SLAB_ROWS), :] = x[:, s * LANES:(s + 1) * LANES]


def _load_slabs(ref, m, first_row=0):
    return jnp.concatenate(
        [ref[pl.ds(first_row + s, m, stride=SLAB_ROWS), :] for s in range(SLAB_ROWS)], axis=1)


def _top_k_route(logits, counts, n_exp):
    tm = logits.shape[0]
    lane = lax.broadcasted_iota(jnp.int32, (tm, LANES), 1).astype(F32)
    lg = jnp.where(lane < n_exp, logits, NEG_INF)
    vals, ids, picks = [], [], []
    for _ in range(TOP_K):
        v = jnp.max(lg, axis=-1, keepdims=True)
        e = jnp.min(jnp.where(lg == v, lane, float(LANES)), axis=-1, keepdims=True)
        pick = lane == e
        lg = jnp.where(pick, NEG_INF, lg)
        vals.append(v), ids.append(e), picks.append(pick)
    exps = [jnp.exp(v - vals[0]) for v in vals]
    total = functools.reduce(lambda a, b: a + b, exps)
    chosen = functools.reduce(lambda a, b: a + b, [jnp.where(p, 1.0, 0.0) for p in picks])
    r = lax.broadcasted_iota(jnp.int32, (tm, tm), 0)
    c = lax.broadcasted_iota(jnp.int32, (tm, tm), 1)
    earlier = jnp.where(c < r, 1.0, 0.0).astype(BF16)
    before = _dot(earlier, chosen.astype(BF16)) + counts
    table = jnp.zeros((tm, LANES), F32)
    for k in range(TOP_K):
        rank = jnp.sum(jnp.where(picks[k], before, 0.0), axis=-1, keepdims=True)
        table = jnp.where(lane == k, exps[k] / total, table)
        table = jnp.where(lane == TOP_K + k, ids[k], table)
        table = jnp.where(lane == 2 * TOP_K + k, rank, table)
    return table, counts + jnp.sum(chosen, axis=0, keepdims=True)


def _xattn_out_kernel(x_ref, o_ref, wo_ref, g_ref, wr_ref, br_ref, cnt_ref,
                      x2_ref, xn_ref, rt_ref, cnt_out_ref, cnt_scr, *, n_exp):
    @pl.when(pl.program_id(0) == 0)
    def _():
        cnt_scr[...] = cnt_ref[...]

    x2 = x_ref[...] + _dot(o_ref[...].astype(BF16), wo_ref[...])
    x2_ref[...] = x2
    xn = _rms_f32(x2, g_ref[...])
    _store_slabs(xn_ref, xn)
    logits = _dot(xn.astype(BF16), wr_ref[...]) + br_ref[...]
    table, counts = _top_k_route(logits, cnt_scr[...], n_exp)
    rt_ref[...] = table
    cnt_scr[...] = counts
    cnt_out_ref[...] = counts


def _xattn_out(x, o, wo, g, wr, br, counts, *, n_exp, tm):
    n, d = x.shape
    assert d == SLAB_ROWS * LANES
    row = lambda i: (i, 0)
    fixed = lambda i: (0, 0)
    return pl.pallas_call(
        functools.partial(_xattn_out_kernel, n_exp=n_exp),
        grid=(n // tm,),
        in_specs=[pl.BlockSpec((tm, d), row), pl.BlockSpec((tm, o.shape[1]), row),
                  pl.BlockSpec(wo.shape, fixed), pl.BlockSpec((1, d), fixed),
                  pl.BlockSpec(wr.shape, fixed), pl.BlockSpec((1, LANES), fixed),
                  pl.BlockSpec((1, LANES), fixed)],
        out_specs=[pl.BlockSpec((tm, d), row), pl.BlockSpec((tm * SLAB_ROWS, LANES), row),
                   pl.BlockSpec((tm, LANES), row), pl.BlockSpec((1, LANES), fixed)],
        out_shape=[jax.ShapeDtypeStruct((n, d), F32), jax.ShapeDtypeStruct((n * SLAB_ROWS, LANES), F32),
                   jax.ShapeDtypeStruct((n, LANES), F32), jax.ShapeDtypeStruct((1, LANES), F32)],
        scratch_shapes=[pltpu.VMEM((1, LANES), F32)],
        compiler_params=_cparams("arbitrary"),
        name="xattn_out",
    )(x, o, wo, g, wr, br, counts)


def _slab(ref, row):
    return ref.at[pl.ds(pl.multiple_of(row * SLAB_ROWS, SLAB_ROWS), SLAB_ROWS)]


def _dispatch_kernel(dest_ref, x_ref, xs_in, xs_hbm, sem):
    del xs_in
    n_tok = dest_ref.shape[1] // TOP_K

    def slab_copies(t):
        src = _slab(x_ref, t)
        return [pltpu.make_async_copy(src, _slab(xs_hbm, dest_ref[0, t * TOP_K + k]), sem)
                for k in range(TOP_K)]

    def start(t, c):
        for cp in slab_copies(t):
            cp.start()
        return c

    def wait(t, c):
        for cp in slab_copies(t):
            cp.wait()
        return c

    lax.fori_loop(0, n_tok, start, 0, unroll=2)
    lax.fori_loop(0, n_tok, wait, 0, unroll=2)


def _dispatch(xn_slabs, dest, xs, *, tc):
    n = xn_slabs.shape[0] // SLAB_ROWS
    return pl.pallas_call(
        _dispatch_kernel,
        grid=(n // tc,),
        in_specs=[pl.BlockSpec((None, 1, tc * TOP_K), lambda i: (i, 0, 0), memory_space=pltpu.SMEM),
                  pl.BlockSpec((tc * SLAB_ROWS, LANES), lambda i: (i, 0)),
                  pl.BlockSpec(memory_space=pl.ANY)],
        out_specs=pl.BlockSpec(memory_space=pl.ANY),
        out_shape=jax.ShapeDtypeStruct(xs.shape, xs.dtype),
        scratch_shapes=[pltpu.SemaphoreType.DMA(())],
        input_output_aliases={2: 0},
        compiler_params=_cparams("arbitrary"),
        name="moe_dispatch",
    )(dest, xn_slabs, xs)


def _for_valid_rows(tile, n_used, valid_ref, tm, compute, zero_all):
    used = tile < n_used
    valid = valid_ref[tile]
    sub = tm // ROW_VARIANTS

    @pl.when(jnp.logical_not(used))
    def _():
        zero_all()

    for i in range(ROW_VARIANTS):
        lo, m = i * sub, (i + 1) * sub
        in_range = valid > lo if m == tm else jnp.logical_and(valid > lo, valid <= m)

        @pl.when(jnp.logical_and(used, in_range))
        def _(m=m):
            compute(m)


def _gate_up_kernel(be_ref, nblk_ref, valid_ref, first_ref, slot_ref, more_ref, ne_ref, nj_ref,
                    x_ref, w_hbm, bg_ref, bl_ref, h_ref, wbuf, sem):
    tm, tn = h_ref.shape
    de = w_hbm.shape[2] // 2
    j = pl.program_id(0)
    r = pl.program_id(1)
    t = j * pl.num_programs(1) + r
    slot = slot_ref[t]

    def weight_copies(e, jj, s):
        return [pltpu.make_async_copy(
            w_hbm.at[e, :, pl.ds(pl.multiple_of(half * de + jj * tn, tn), tn)], wbuf.at[s, half], sem.at[s, half])
            for half in range(2)]

    e_cur = be_ref[jnp.minimum(r, nblk_ref[0] - 1)]

    @pl.when(t == 0)
    def _():
        for cp in weight_copies(e_cur, 0, 0):
            cp.start()

    @pl.when(first_ref[t] == 1)
    def _():
        for cp in weight_copies(e_cur, j, slot):
            cp.wait()

        @pl.when(more_ref[t] == 1)
        def _():
            for cp in weight_copies(ne_ref[t], nj_ref[t], 1 - slot):
                cp.start()

    def zero_all():
        h_ref[...] = jnp.zeros_like(h_ref)

    def compute(m):
        x = _load_slabs(x_ref, m).astype(BF16)
        gate = jnp.minimum(_dot(x, wbuf[slot, 0].astype(BF16)) + bg_ref[...], SWIGLU_LIMIT)
        lin = jnp.clip(_dot(x, wbuf[slot, 1].astype(BF16)) + bl_ref[...], -SWIGLU_LIMIT, SWIGLU_LIMIT)
        h_ref[:m, :] = ((lin + 1.0) * gate * jax.nn.sigmoid(SWIGLU_ALPHA * gate)).astype(h_ref.dtype)
        if m < tm:
            h_ref[m:, :] = jnp.zeros((tm - m, h_ref.shape[1]), h_ref.dtype)

    _for_valid_rows(pl.program_id(1), nblk_ref[0], valid_ref, tm, compute, zero_all)


def _gate_up(xs, w_gu, b_gu, blk_e, nblk, valid, *, tm, tn):
    rows = xs.shape[0] // SLAB_ROWS
    n_exp, d, de2 = w_gu.shape
    de = de2 // 2
    nj = de // tn
    n_r = rows // tm
    used = lambda j, r, be, nb, *_: (jnp.minimum(r, nb[0] - 1), 0)
    exp_col = lambda off: (lambda j, r, be, nb, *_: (be[jnp.minimum(r, nb[0] - 1)], 0, j + off))
    r_idx = jnp.arange(n_r, dtype=jnp.int32)
    e_r = blk_e[jnp.minimum(r_idx, nblk[0] - 1)]
    first_r = jnp.logical_or(r_idx == 0, jnp.logical_and(r_idx < nblk[0], e_r != jnp.roll(e_r, 1)))
    first = jnp.tile(first_r, nj)
    e_flat = jnp.tile(e_r, nj)
    j_flat = jnp.repeat(jnp.arange(nj, dtype=jnp.int32), n_r)
    n_flat = nj * n_r
    s_idx = jnp.arange(n_flat, dtype=jnp.int32)
    slot = (jnp.cumsum(first.astype(jnp.int32)) - 1) % 2
    later_first = jnp.logical_and(s_idx[None, :] > s_idx[:, None], first[None, :])
    nxt = jnp.min(jnp.where(later_first, s_idx[None, :], n_flat), axis=1)
    at_nxt = s_idx[None, :] == nxt[:, None]
    nxt_e = jnp.sum(jnp.where(at_nxt, e_flat[None, :], 0), axis=1)
    nxt_j = jnp.sum(jnp.where(at_nxt, j_flat[None, :], 0), axis=1)
    grid_spec = pltpu.PrefetchScalarGridSpec(
        num_scalar_prefetch=8,
        grid=(nj, n_r),
        in_specs=[pl.BlockSpec((tm * SLAB_ROWS, LANES), used),
                  pl.BlockSpec(memory_space=pl.ANY),
                  pl.BlockSpec((None, 1, tn), exp_col(0)), pl.BlockSpec((None, 1, tn), exp_col(nj))],
        out_specs=pl.BlockSpec((tm, tn), lambda j, r, *_: (r, j)),
        scratch_shapes=[pltpu.VMEM((2, 2, d, tn), F32), pltpu.SemaphoreType.DMA((2, 2))],
    )
    b3 = b_gu.reshape(n_exp, 1, de2)
    i32 = lambda a: a.astype(jnp.int32)
    return pl.pallas_call(
        _gate_up_kernel,
        grid_spec=grid_spec,
        out_shape=jax.ShapeDtypeStruct((rows, de), BF16),
        compiler_params=_cparams("arbitrary", "arbitrary"),
        name="moe_gate_up",
    )(blk_e, nblk, valid, i32(first), i32(slot), i32(nxt < n_flat), i32(nxt_e), i32(nxt_j), xs, w_gu, b3, b3)


def _down_kernel(be_ref, nblk_ref, valid_ref, h_ref, w_ref, b_ref, o_ref):
    del be_ref
    tm = h_ref.shape[0]

    def zero_all():
        o_ref[...] = jnp.zeros_like(o_ref)

    def compute(m):
        out = _dot(h_ref[:m, :], w_ref[...].astype(BF16)) + b_ref[...]
        _store_slabs(o_ref, out)
        if m < tm:
            o_ref[m * SLAB_ROWS:, :] = jnp.zeros(((tm - m) * SLAB_ROWS, LANES), o_ref.dtype)

    _for_valid_rows(pl.program_id(0), nblk_ref[0], valid_ref, tm, compute, zero_all)


def _down(hdn, w_down, b_down, blk_e, nblk, valid, *, tm):
    rows, de = hdn.shape
    n_exp, _, d = w_down.shape
    assert d == SLAB_ROWS * LANES
    used = lambda r, be, nb, vl: (jnp.minimum(r, nb[0] - 1), 0)
    expert = lambda r, be, nb, vl: (be[jnp.minimum(r, nb[0] - 1)], 0, 0)
    grid_spec = pltpu.PrefetchScalarGridSpec(
        num_scalar_prefetch=3,
        grid=(rows // tm,),
        in_specs=[pl.BlockSpec((tm, de), used),
                  pl.BlockSpec((None, de, d), expert), pl.BlockSpec((None, 1, d), expert)],
        out_specs=pl.BlockSpec((tm * SLAB_ROWS, LANES), lambda r, be, nb, vl: (r, 0)),
    )
    return pl.pallas_call(
        _down_kernel,
        grid_spec=grid_spec,
        out_shape=jax.ShapeDtypeStruct((rows * SLAB_ROWS, LANES), F32),
        compiler_params=_cparams("arbitrary"),
        name="moe_down",
    )(blk_e, nblk, valid, hdn, w_down, b_down.reshape(n_exp, 1, d))


def _combine_kernel(pos_ref, x_ref, gate_ref, rows_hbm, g_ref, y_ref, buf, sem):
    tc = x_ref.shape[0]
    hc = tc // 2
    per_half = TOP_K * hc

    def slab_copy(half, i):
        return pltpu.make_async_copy(_slab(rows_hbm, pos_ref[0, half * per_half + i]),
                                     _slab(buf.at[half], i), sem.at[half])

    for half in range(2):
        lax.fori_loop(0, per_half, lambda i, c, half=half: (slab_copy(half, i).start(), c)[1], 0, unroll=8)
    for half in range(2):
        lax.fori_loop(0, per_half, lambda i, c, half=half: (slab_copy(half, i).wait(), c)[1], 0, unroll=8)
        rows = slice(half * hc, (half + 1) * hc)
        x3 = x_ref[rows, :]
        for k in range(TOP_K):
            x3 = x3 + gate_ref[rows, k:k + 1] * _load_slabs(buf.at[half], hc, first_row=k * hc * SLAB_ROWS)
        y_ref[rows, :] = _rms_f32(x3, g_ref[...])


def _combine(x, gates, rows, pos, g, *, tc):
    n, d = x.shape
    return pl.pallas_call(
        _combine_kernel,
        grid=(n // tc,),
        in_specs=[pl.BlockSpec((None, 1, TOP_K * tc), lambda i: (i, 0, 0), memory_space=pltpu.SMEM),
                  pl.BlockSpec((tc, d), lambda i: (i, 0)),
                  pl.BlockSpec((tc, TOP_K), lambda i: (i, 0)),
                  pl.BlockSpec(memory_space=pl.ANY),
                  pl.BlockSpec((1, d), lambda i: (0, 0))],
        out_specs=pl.BlockSpec((tc, d), lambda i: (i, 0)),
        out_shape=jax.ShapeDtypeStruct((n, d), F32),
        scratch_shapes=[pltpu.VMEM((2, TOP_K * (tc // 2) * SLAB_ROWS, LANES), F32),
                        pltpu.SemaphoreType.DMA((2,))],
        compiler_params=_cparams("arbitrary"),
        name="moe_combine",
    )(pos, x, gates, rows, g)


def _row_layout(table, counts, n_exp, tm):
    n_tok = table.shape[0]
    gates = table[:, :TOP_K]
    expert = table[:, TOP_K:2 * TOP_K].astype(jnp.int32)
    rank = table[:, 2 * TOP_K:3 * TOP_K].astype(jnp.int32)
    counts = counts.astype(jnp.int32)
    padded = (counts + tm - 1) // tm * tm
    pad_ends = jnp.cumsum(padded)
    pad_starts = pad_ends - padded
    is_e = expert[:, :, None] == jnp.arange(n_exp, dtype=jnp.int32)[None, None, :]
    dest = rank + jnp.sum(jnp.where(is_e, pad_starts[None, None, :], 0), axis=-1)
    n_blocks = -(-n_tok * TOP_K // tm) + n_exp
    tile_start = jnp.arange(n_blocks, dtype=jnp.int32) * tm
    blk_e = jnp.minimum(jnp.sum((pad_ends[None, :] <= tile_start[:, None]).astype(jnp.int32), axis=1),
                        n_exp - 1)
    nblk = (pad_ends[-1] // tm).reshape(1)
    is_blk_e = blk_e[:, None] == jnp.arange(n_exp, dtype=jnp.int32)[None, :]
    real_end = jnp.sum(jnp.where(is_blk_e, (pad_starts + counts)[None, :], 0), axis=1)
    valid = jnp.clip(real_end - tile_start, 0, tm).astype(jnp.int32)
    return gates, dest, blk_e, nblk, valid, n_blocks


def _moe_final(x2_groups, xn_groups, table_groups, counts, w_gu, b_gu, w_down, b_down, g_final):
    n_exp = w_gu.shape[0]
    tm = min(EXPERT_TILE, TOP_K * sum(x.shape[0] for x in x2_groups))
    gates, dest, blk_e, nblk, valid, n_blocks = _row_layout(
        jnp.concatenate(table_groups, axis=0), counts[0, :n_exp], n_exp, tm)
    xs = jnp.zeros((n_blocks * tm * SLAB_ROWS, LANES), F32)
    start = 0
    for xn in xn_groups:
        n = xn.shape[0] // SLAB_ROWS
        tc = min(COMBINE_TILE, n)
        xs = _dispatch(xn, dest[start:start + n].reshape(n // tc, 1, tc * TOP_K), xs, tc=tc)
        start += n
    hdn = _gate_up(xs, w_gu, b_gu, blk_e, nblk, valid, tm=tm, tn=min(1024, w_gu.shape[2] // 2))
    rows = _down(hdn, w_down, b_down, blk_e, nblk, valid, tm=tm)
    outs, start = [], 0
    for x2 in x2_groups:
        n = x2.shape[0]
        tc = min(COMBINE_TILE, n)
        pos = dest[start:start + n].reshape(n // tc, 2, tc // 2, TOP_K).transpose(0, 1, 3, 2)
        outs.append(_combine(x2, gates[start:start + n], rows, pos.reshape(n // tc, 1, TOP_K * tc),
                             g_final, tc=tc))
        start += n
    return outs


def _layer_common(x, attn, pool, wa, wp, g_x, wq, mk, mv, rows_per_mem, xattn_tile,
                  wo, g_f, wr, br, counts, n_exp):
    n = x.shape[0]
    x1, qx = _mix_out(x, attn, pool, wa, wp, g_x, wq, tm=min(256, n))
    o = _xattn(qx, mk, mv, rows_per_mem=rows_per_mem, n_mem=mk.shape[0] * rows_per_mem // n,
               tm=xattn_tile)
    return _xattn_out(x1, o, wo, g_f, wr, br, counts, n_exp=n_exp, tm=min(256, n))


def kernel(x_prompt, x_sample, mem_prompt, cache_k, cache_v, cache_logf, cache_mem_k, cache_mem_v,
           state_pool, page_table, g_mix, w_in, b_forget, w_pool, s_pool, w_out, g_xattn, g_mem,
           w_xq, w_xk, w_xv, w_xo, g_ffn, w_router, b_router, w_gu, b_gu, w_down, b_down, g_final):
    depth = w_in.shape[0]
    bp, tp, d = x_prompt.shape
    bs, ts, _ = x_sample.shape
    n_pool, page, heads, hd = cache_k.shape[1:]
    fw = heads * hd
    pw = state_pool.shape[-1]
    n_mem = mem_prompt.shape[1]
    xw = w_xq.shape[-1]
    n_exp = w_router.shape[-1]
    past_len = page_table.shape[1] * page
    assert hd == HEAD_DIM and page == PAGE_SIZE and heads == 8 and pw == fw
    assert w_in.shape[-1] == 3 * fw + heads + pw and state_pool.shape[2] == POOL_STATE
    assert depth == 1, "the experts of all groups are evaluated together after the last layer"

    xp = x_prompt.reshape(bp * tp, d)
    xs = x_sample.reshape(bs * ts, d)
    row = lambda a: a.reshape(1, -1)
    outs = {k: [] for k in ("kp", "vp", "lfp", "pp", "mkp", "mvp", "ks", "vs", "lfs", "ps")}
    for l in range(depth):
        w = w_in[l]
        w4 = jnp.concatenate([w[:, :3 * fw], w[:, 3 * fw + heads:]], axis=1).astype(BF16)
        wf = jnp.pad(w[:, 3 * fw:3 * fw + heads], ((0, 0), (0, LANES - heads))).astype(BF16)
        bfp = jnp.pad(b_forget[l], (0, LANES - heads)).reshape(1, LANES)
        wpool = w_pool[l].astype(BF16)
        spool = row(s_pool[l])
        wa = w_out[l, :fw].astype(BF16)
        wp = w_out[l, fw:].astype(BF16)
        wq = w_xq[l].astype(BF16)
        wkv = jnp.concatenate([w_xk[l], w_xv[l]], axis=1).astype(BF16)
        wo = w_xo[l].astype(BF16)
        wr = jnp.pad(w_router[l], ((0, 0), (0, LANES - n_exp))).astype(BF16)
        br = jnp.pad(b_router[l], (0, LANES - n_exp)).reshape(1, LANES)

        tm = min(512, tp)
        q, k, v, u, kb, vb, lf, ct = _in_proj(xp, row(g_mix[l]), w4, wf, bfp, seq_len=tp, heads=heads, tm=tm)
        attn = _fox_prompt(q, kb, vb, ct, batch=bp, seq_len=tp, heads=heads, tq=tm,
                           heads_per_step=FLASH_HEADS_PER_STEP)
        pool = _pool_prompt(u, wpool, spool, seq_len=tp, tm=tm)
        outs["kp"].append(k.reshape(bp, tp, heads, hd))
        outs["vp"].append(v.reshape(bp, tp, heads, hd))
        outs["lfp"].append(lf.reshape(bp, tp, heads))
        outs["pp"].append(u.reshape(bp, tp, pw)[:, tp - POOL_STATE:])
        mk, mv = _mem_kv(mem_prompt.reshape(bp * n_mem, d), row(g_mem[l]), wkv, tm=min(256, bp * n_mem))
        outs["mkp"].append(mk.reshape(bp, n_mem, xw // HEAD_DIM, HEAD_DIM))
        outs["mvp"].append(mv.reshape(bp, n_mem, xw // HEAD_DIM, HEAD_DIM))
        x2p, xnp_, rtp, counts = _layer_common(xp, attn, pool, wa, wp, row(g_xattn[l]), wq, mk, mv, tp, tm,
                                               wo, row(g_ffn[l]), wr, br, jnp.zeros((1, LANES), F32), n_exp)

        ns = bs * ts
        tms = min(512, ns)
        q, k, v, u, _, _, lf, ct = _in_proj(xs, row(g_mix[l]), w4, wf, bfp, seq_len=ts, heads=heads, tm=tms)
        c_new = ct.reshape(ns // tms, heads, tms // ts, ts).transpose(0, 2, 3, 1).reshape(bs, 1, ts * heads)
        c_new = jnp.pad(c_new, ((0, 0), (0, 0), (0, LANES - ts * heads)))
        scan = _logf_scan(cache_logf[l].reshape(n_pool, page * heads), heads=heads,
                          tm=_divisor_tile(n_pool, 512))
        attn = _fox_decode(q, k.reshape(bs, ts * heads, hd), v.reshape(bs, ts * heads, hd), c_new,
                           cache_k[l].reshape(n_pool, page * heads, hd),
                           cache_v[l].reshape(n_pool, page * heads, hd),
                           scan, page_table, heads=heads, n_new=ts,
                           pages_per_step=_divisor_tile(page_table.shape[1], DECODE_PAGES_PER_STEP, 1))
        u3 = u.reshape(bs, ts, pw)
        pool = _pool_sample(u3.transpose(1, 0, 2), state_pool[l].transpose(1, 0, 2), wpool, spool,
                            past_len=past_len)
        pool = pool.transpose(1, 0, 2).reshape(ns, pw)
        outs["ks"].append(k.reshape(bs, ts, heads, hd))
        outs["vs"].append(v.reshape(bs, ts, heads, hd))
        outs["lfs"].append(lf.reshape(bs, ts, heads))
        outs["ps"].append(jnp.concatenate([state_pool[l], u3], axis=1)[:, ts:])
        mks = cache_mem_k[l].reshape(bs * n_mem, xw)
        mvs = cache_mem_v[l].reshape(bs * n_mem, xw)
        x2s, xns, rts, counts = _layer_common(xs, attn, pool, wa, wp, row(g_xattn[l]), wq, mks, mvs, ts,
                                              ts * _divisor_tile(bs, XATTN_SEQS_PER_STEP, 1),
                                              wo, row(g_ffn[l]), wr, br, counts, n_exp)

        yp, ys = _moe_final([x2p, x2s], [xnp_, xns], [rtp, rts], counts, w_gu[l], b_gu[l], w_down[l],
                            b_down[l], row(g_final))

    st = lambda name: jnp.stack(outs[name])
    return (yp.reshape(bp, tp, d), ys.reshape(bs, ts, d),
            st("kp"), st("vp"), st("lfp"), st("pp"), st("mkp"), st("mvp"),
            st("ks"), st("vs"), st("lfs"), st("ps"))
```
